```python
import math
import jax
import jax.numpy as jnp
from jax import lax
import numpy as np


D_MODEL = 2048
BATCH = 4
SEQ = 2048
DEPTH = 4

N_MIXERS = 2
N_SSD_LAYERS = (DEPTH + 1) // 2
N_POOL_LAYERS = DEPTH // 2
EPS = 1e-6

D_FF = 5632

SSD_EXPAND = 2
D_INNER = SSD_EXPAND * D_MODEL
SSD_HEAD_DIM = 64
SSD_HEADS = D_INNER // SSD_HEAD_DIM
SSD_GROUPS = 8
SSD_HEADS_PER_GROUP = SSD_HEADS // SSD_GROUPS
SSD_STATE = 128
SSD_CONV_W = 5
SSD_CHUNK = 128
SSD_CONV_DIM = D_INNER + 2 * SSD_GROUPS * SSD_STATE
SSD_IN_DIM = D_INNER + SSD_CONV_DIM + 2 * SSD_HEADS

POOL_WINDOWS = (2, 4, 8, 16)
N_POOL_GROUPS = 4
D_POOL = D_MODEL
POOL_GROUP_DIM = D_POOL // N_POOL_GROUPS

kernel_name = "hybrid_ssd_pool_macaron_encoder"


def rmsnorm(x, g):
    xf = x.astype(jnp.float32)
    y = xf * lax.rsqrt(jnp.mean(xf * xf, axis=-1, keepdims=True) + EPS)
    return (y * g.astype(jnp.float32)).astype(x.dtype)


def swiglu(h, w_gate, w_up, w_down):
    return (jax.nn.silu(h @ w_gate) * (h @ w_up)) @ w_down


def centred_dwconv(u, w, bias):
    pad = SSD_CONV_W // 2
    out = lax.conv_general_dilated(
        u, w[:, None, :].astype(u.dtype), window_strides=(1,), padding=[(pad, pad)],
        dimension_numbers=("NWC", "WIO", "NWC"), feature_group_count=u.shape[-1])
    return out + bias.astype(u.dtype)


def ssd_chunked(xh, dt, a, bm, cm):
    b, l, h, p = xh.shape
    q = SSD_CHUNK
    nc = l // q
    g, hg, n = SSD_GROUPS, SSD_HEADS_PER_GROUP, SSD_STATE
    xdt = (xh * dt[..., None].astype(xh.dtype)).reshape(b, nc, q, g, hg, p)
    bm = bm.reshape(b, nc, q, g, n)
    cm = cm.reshape(b, nc, q, g, n)
    a_cs = jnp.cumsum((dt * a).reshape(b, nc, q, g, hg), axis=2)
    tri = jnp.tril(jnp.ones((q, q), dtype=bool))[:, :, None, None]
    seg = a_cs[:, :, :, None] - a_cs[:, :, None, :]
    lmat = jnp.exp(jnp.where(tri, seg, -jnp.inf))
    cb = jnp.einsum("bcign,bcjgn->bcgij", cm, bm)
    y_diag = jnp.einsum("bcgij,bcijgh,bcjghp->bcighp", cb, lmat, xdt)
    decay_states = jnp.exp(a_cs[:, :, -1:] - a_cs)
    states = jnp.einsum("bcjgn,bcjgh,bcjghp->bcghpn", bm, decay_states, xdt)
    chunk_decay = jnp.exp(a_cs[:, :, -1])

    def step(hstate, inp):
        st, dec = inp
        return hstate * dec[..., None, None] + st, hstate

    h0 = jnp.zeros((b, g, hg, p, n), states.dtype)
    _, h_in = lax.scan(step, h0, (jnp.moveaxis(states, 1, 0), jnp.moveaxis(chunk_decay, 1, 0)))
    h_in = jnp.moveaxis(h_in, 0, 1)
    y_off = jnp.einsum("bcign,bcigh,bcghpn->bcighp", cm, jnp.exp(a_cs), h_in)
    return (y_diag + y_off).reshape(b, l, h, p).astype(xh.dtype)


def ssd_mixer(h, w_in, conv_w, conv_b, dt_bias, a_log, d_skip, norm_g, w_out):
    b, l, _ = h.shape
    proj = h @ w_in
    z, xbc, dt_raw = jnp.split(proj, [D_INNER, D_INNER + SSD_CONV_DIM], axis=-1)
    xbc = jax.nn.silu(centred_dwconv(xbc, conv_w, conv_b))
    xs, bm, cm = jnp.split(xbc, [D_INNER, D_INNER + SSD_GROUPS * SSD_STATE], axis=-1)
    xh = xs.reshape(b, l, SSD_HEADS, SSD_HEAD_DIM)
    bm = bm.reshape(b, l, SSD_GROUPS, SSD_STATE)
    cm = cm.reshape(b, l, SSD_GROUPS, SSD_STATE)
    dt = jax.nn.softplus(dt_raw.astype(jnp.float32).reshape(b, l, 2, SSD_HEADS)
                         + dt_bias.astype(jnp.float32))
    a = -jnp.exp(a_log.astype(jnp.float32))
    y_fwd = ssd_chunked(xh, dt[:, :, 0], a[0], bm, cm)
    y_bwd = jnp.flip(ssd_chunked(jnp.flip(xh, 1), jnp.flip(dt[:, :, 1], 1), a[1],
                                 jnp.flip(bm, 1), jnp.flip(cm, 1)), axis=1)
    y = y_fwd + y_bwd + xh * d_skip[:, None].astype(xh.dtype)
    y = y.reshape(b, l, D_INNER)
    y = rmsnorm(y * jax.nn.silu(z), norm_g)
    return y @ w_out


def pool_mixer(h, w_in, w_group, scale, w_out):
    b, l, _ = h.shape
    u = (h @ w_in).reshape(b, l, N_POOL_GROUPS, POOL_GROUP_DIM)
    uf = u.astype(jnp.float32)
    cs = jnp.pad(jnp.cumsum(uf, axis=1), ((0, 0), (1, 0), (0, 0), (0, 0)))
    t = jnp.arange(l)
    pooled = []
    for gi, w in enumerate(POOL_WINDOWS):
        lo = jnp.clip(t - w // 2, 0, l)
        hi = jnp.clip(t + w // 2, 0, l)
        cs_g = cs[:, :, gi]
        cnt = (hi - lo).astype(jnp.float32)[None, :, None]
        pooled.append((cs_g[:, hi] - cs_g[:, lo]) / cnt)
    pooled = jnp.stack(pooled, axis=2)
    mix = (pooled - uf).astype(h.dtype)
    v = jnp.einsum("blgc,gcd->blgd", mix, w_group).reshape(b, l, D_POOL)
    return (v * scale) @ w_out


def setup_inputs(seed: int = 0) -> dict:
    key = jax.random.key(seed)
    ks = jax.random.split(key, 24)
    f32 = jnp.float32

    def nrm(k, shape, fan_in):
        return jax.random.normal(k, shape, f32) * (fan_in ** -0.5)

    def gain(k, shape):
        return 1.0 + 0.02 * jax.random.normal(k, shape, f32)

    x = jax.random.normal(ks[0], (BATCH, SEQ, D_MODEL), f32)
    ffn_norm = gain(ks[1], (DEPTH, 2, D_MODEL))
    ffn_w_gate = nrm(ks[2], (DEPTH, 2, D_MODEL, D_FF), D_MODEL)
    ffn_w_up = nrm(ks[3], (DEPTH, 2, D_MODEL, D_FF), D_MODEL)
    ffn_w_down = nrm(ks[4], (DEPTH, 2, D_FF, D_MODEL), D_FF)
    mix_norm = gain(ks[5], (DEPTH, D_MODEL))

    ssd_w_in = nrm(ks[6], (N_SSD_LAYERS, D_MODEL, SSD_IN_DIM), D_MODEL)
    ssd_conv_w = nrm(ks[7], (N_SSD_LAYERS, SSD_CONV_W, SSD_CONV_DIM), SSD_CONV_W)
    ssd_conv_b = 0.02 * jax.random.normal(ks[8], (N_SSD_LAYERS, SSD_CONV_DIM), f32)
    dt0 = jnp.exp(jax.random.uniform(ks[9], (N_SSD_LAYERS, 2, SSD_HEADS), f32,
                                     minval=math.log(1e-3), maxval=math.log(1e-1)))
    ssd_dt_bias = dt0 + jnp.log(-jnp.expm1(-dt0))
    ssd_a_log = jnp.log(jax.random.uniform(ks[10], (N_SSD_LAYERS, 2, SSD_HEADS), f32,
                                           minval=1.0, maxval=16.0))
    ssd_d = 1.0 + 0.1 * jax.random.normal(ks[11], (N_SSD_LAYERS, SSD_HEADS), f32)
    ssd_norm = gain(ks[12], (N_SSD_LAYERS, D_INNER))
    ssd_w_out = nrm(ks[13], (N_SSD_LAYERS, D_INNER, D_MODEL), D_INNER)

    pool_w_in = nrm(ks[14], (N_POOL_LAYERS, D_MODEL, D_POOL), D_MODEL)
    pool_w_group = nrm(ks[15], (N_POOL_LAYERS, N_POOL_GROUPS, POOL_GROUP_DIM, POOL_GROUP_DIM),
                       POOL_GROUP_DIM)
    pool_scale = 1.0 + 0.1 * jax.random.normal(ks[16], (N_POOL_LAYERS, D_POOL), f32)
    pool_w_out = nrm(ks[17], (N_POOL_LAYERS, D_POOL, D_MODEL), D_POOL)

    final_norm = gain(ks[18], (D_MODEL,))
    return {
        "x": x, "ffn_norm": ffn_norm, "ffn_w_gate": ffn_w_gate, "ffn_w_up": ffn_w_up,
        "ffn_w_down": ffn_w_down, "mix_norm": mix_norm,
        "ssd_w_in": ssd_w_in, "ssd_conv_w": ssd_conv_w, "ssd_conv_b": ssd_conv_b,
        "ssd_dt_bias": ssd_dt_bias, "ssd_a_log": ssd_a_log, "ssd_d": ssd_d,
        "ssd_norm": ssd_norm, "ssd_w_out": ssd_w_out,
        "pool_w_in": pool_w_in, "pool_w_group": pool_w_group, "pool_scale": pool_scale,
        "pool_w_out": pool_w_out, "final_norm": final_norm,
    }


def reference(x, ffn_norm, ffn_w_gate, ffn_w_up, ffn_w_down, mix_norm,
              ssd_w_in, ssd_conv_w, ssd_conv_b, ssd_dt_bias, ssd_a_log, ssd_d,
              ssd_norm, ssd_w_out, pool_w_in, pool_w_group, pool_scale, pool_w_out,
              final_norm):
    for i in range(DEPTH):
        x = x + 0.5 * swiglu(rmsnorm(x, ffn_norm[i, 0]), ffn_w_gate[i, 0], ffn_w_up[i, 0],
                             ffn_w_down[i, 0])
        h = rmsnorm(x, mix_norm[i])
        j = i // N_MIXERS
        if i % N_MIXERS == 0:
            x = x + ssd_mixer(h, ssd_w_in[j], ssd_conv_w[j], ssd_conv_b[j], ssd_dt_bias[j],
                              ssd_a_log[j], ssd_d[j], ssd_norm[j], ssd_w_out[j])
        else:
            x = x + pool_mixer(h, pool_w_in[j], pool_w_group[j], pool_scale[j], pool_w_out[j])
        x = x + 0.5 * swiglu(rmsnorm(x, ffn_norm[i, 1]), ffn_w_gate[i, 1], ffn_w_up[i, 1],
                             ffn_w_down[i, 1])
    return rmsnorm(x, final_norm)
```

```python
import functools

import jax
import jax.numpy as jnp
from jax import lax
from jax.experimental import pallas as pl
from jax.experimental.pallas import tpu as pltpu

F32 = jnp.float32
BF16 = jnp.bfloat16

EPS = 1e-6
D_MODEL = 2048
D_FF = 5632
D_INNER = 2 * D_MODEL
SSD_HEAD_DIM = 64
SSD_HEADS = D_INNER // SSD_HEAD_DIM
SSD_GROUPS = 8
SSD_HEADS_PER_GROUP = SSD_HEADS // SSD_GROUPS
SSD_GROUP_DIM = SSD_HEADS_PER_GROUP * SSD_HEAD_DIM
SSD_STATE = 128
SSD_CONV_W = 5
SSD_CHUNK = 128
SSD_BC_DIM = SSD_GROUPS * SSD_STATE
SSD_ZXBC_DIM = 2 * D_INNER + 2 * SSD_BC_DIM
POOL_WINDOWS = (2, 4, 8, 16)
POOL_GROUP_DIM = D_MODEL // len(POOL_WINDOWS)

V7X_VMEM_BYTES = 64 * 1024 * 1024
V7X_LANES = 128
V7X_SUBLANES = 8
CONV_PAD_ROWS = V7X_SUBLANES


def _params(semantics, vmem_bytes):
    return pltpu.CompilerParams(dimension_semantics=semantics,
                                vmem_limit_bytes=min(int(vmem_bytes), V7X_VMEM_BYTES))


def _nbytes(shape, dtype):
    n = 1
    for s in shape:
        n *= s
    return n * jnp.dtype(dtype).itemsize


def _rmsnorm_rows(x, g):
    ms = jnp.mean(x * x, axis=-1, keepdims=True)
    return x * lax.rsqrt(ms + EPS) * g


def _silu(v):
    return v * jax.nn.sigmoid(v)


def _rms_matmul_kernel(x_ref, g_ref, w_ref, o_ref, xn_ref):
    @pl.when(pl.program_id(1) == 0)
    def _():
        xn_ref[...] = _rmsnorm_rows(x_ref[...], g_ref[...]).astype(BF16)

    o_ref[...] = jnp.dot(xn_ref[...], w_ref[...], preferred_element_type=F32).astype(o_ref.dtype)


def _rms_matmul(x, g, w, *, tm, tn, out_dtype):
    t, d = x.shape
    n = w.shape[1]
    vmem = (2 * (_nbytes((tm, d), F32) + _nbytes((d, tn), BF16) + _nbytes((tm, tn), out_dtype))
            + _nbytes((tm, d), BF16) + _nbytes((tm, d), F32) + _nbytes((tm, tn), F32))
    return pl.pallas_call(
        _rms_matmul_kernel,
        grid=(t // tm, n // tn),
        in_specs=[pl.BlockSpec((tm, d), lambda i, j: (i, 0)),
                  pl.BlockSpec((1, d), lambda i, j: (0, 0)),
                  pl.BlockSpec((d, tn), lambda i, j: (0, j))],
        out_specs=pl.BlockSpec((tm, tn), lambda i, j: (i, j)),
        out_shape=jax.ShapeDtypeStruct((t, n), out_dtype),
        scratch_shapes=[pltpu.VMEM((tm, d), BF16)],
        compiler_params=_params(("parallel", "arbitrary"), vmem),
        name="rms_matmul",
    )(x, g.reshape(1, d), w)


def _rms_matmul_t_kernel(x_ref, g_ref, wt_ref, o_ref):
    xn = _rmsnorm_rows(x_ref[...], g_ref[...]).astype(BF16)
    o_ref[...] = lax.dot_general(wt_ref[...], xn, (((1,), (1,)), ((), ())),
                                 preferred_element_type=F32)


def _rms_matmul_t(x, g, wt, *, tm):
    t, d = x.shape
    n = wt.shape[0]
    vmem = (2 * (_nbytes((tm, d), F32) + _nbytes((n, d), BF16) + _nbytes((n, tm), F32))
            + 2 * _nbytes((tm, d), F32))
    return pl.pallas_call(
        _rms_matmul_t_kernel,
        grid=(t // tm,),
        in_specs=[pl.BlockSpec((tm, d), lambda i: (i, 0)),
                  pl.BlockSpec((1, d), lambda i: (0, 0)),
                  pl.BlockSpec((n, d), lambda i: (0, 0))],
        out_specs=pl.BlockSpec((n, tm), lambda i: (0, i)),
        out_shape=jax.ShapeDtypeStruct((n, t), F32),
        compiler_params=_params(("parallel",), vmem),
        name="rms_matmul_t",
    )(x, g.reshape(1, d), wt)


def _ffn_kernel(x_ref, g_ref, wg_ref, wu_ref, wd_ref, o_ref, xn_ref):
    @pl.when(pl.program_id(1) == 0)
    def _():
        x = x_ref[...]
        xn_ref[...] = _rmsnorm_rows(x, g_ref[...]).astype(BF16)
        o_ref[...] = x

    xn = xn_ref[...]
    gate = jnp.dot(xn, wg_ref[...], preferred_element_type=F32)
    up = jnp.dot(xn, wu_ref[...], preferred_element_type=F32)
    hidden = (_silu(gate) * up * 0.5).astype(BF16)
    o_ref[...] += jnp.dot(hidden, wd_ref[...], preferred_element_type=F32)


def _ffn(x, g, wg, wu, wd, *, tm, tf):
    t, d = x.shape
    f = wg.shape[1]
    vmem = (2 * (2 * _nbytes((tm, d), F32) + 2 * _nbytes((d, tf), BF16) + _nbytes((tf, d), BF16))
            + _nbytes((tm, d), BF16) + 3 * _nbytes((tm, tf), F32) + _nbytes((tm, d), F32))
    return pl.pallas_call(
        _ffn_kernel,
        grid=(t // tm, f // tf),
        in_specs=[pl.BlockSpec((tm, d), lambda i, j: (i, 0)),
                  pl.BlockSpec((1, d), lambda i, j: (0, 0)),
                  pl.BlockSpec((d, tf), lambda i, j: (0, j)),
                  pl.BlockSpec((d, tf), lambda i, j: (0, j)),
                  pl.BlockSpec((tf, d), lambda i, j: (j, 0))],
        out_specs=pl.BlockSpec((tm, d), lambda i, j: (i, 0)),
        out_shape=jax.ShapeDtypeStruct((t, d), F32),
        scratch_shapes=[pltpu.VMEM((tm, d), BF16)],
        compiler_params=_params(("parallel", "arbitrary"), vmem),
        name="ffn",
    )(x, g.reshape(1, d), wg, wu, wd)


def _matmul_res_kernel(a_ref, w_ref, r_ref, o_ref):
    o_ref[...] = r_ref[...] + jnp.dot(a_ref[...], w_ref[...], preferred_element_type=F32)


def _matmul_res(a, w, res, *, tm, tn):
    t, k = a.shape
    n = w.shape[1]
    vmem = 2 * (_nbytes((tm, k), a.dtype) + _nbytes((k, tn), BF16) + 2 * _nbytes((tm, tn), F32)) \
        + _nbytes((tm, tn), F32)
    return pl.pallas_call(
        _matmul_res_kernel,
        grid=(t // tm, n // tn),
        in_specs=[pl.BlockSpec((tm, k), lambda i, j: (i, 0)),
                  pl.BlockSpec((k, tn), lambda i, j: (0, j)),
                  pl.BlockSpec((tm, tn), lambda i, j: (i, j))],
        out_specs=pl.BlockSpec((tm, tn), lambda i, j: (i, j)),
        out_shape=jax.ShapeDtypeStruct((t, n), F32),
        compiler_params=_params(("parallel", "arbitrary"), vmem),
        name="matmul_res",
    )(a, w, res)


def _norm_matmul_res_kernel(yg_ref, ssq_ref, g_ref, w_ref, r_ref, o_ref, an_ref):
    @pl.when(pl.program_id(1) == 0)
    def _():
        ms = jnp.sum(ssq_ref[...], axis=-1, keepdims=True) * (1.0 / yg_ref.shape[-1])
        an_ref[...] = (yg_ref[...] * lax.rsqrt(ms + EPS) * g_ref[...]).astype(BF16)

    o_ref[...] = r_ref[...] + jnp.dot(an_ref[...], w_ref[...], preferred_element_type=F32)


def _norm_matmul_res(yg, ssq, g, w, res, *, tm, tn):
    t, k = yg.shape
    n = w.shape[1]
    s = ssq.shape[1]
    vmem = (2 * (_nbytes((tm, k), F32) + _nbytes((tm, s), F32) + _nbytes((k, tn), BF16)
                 + 2 * _nbytes((tm, tn), F32))
            + _nbytes((tm, k), BF16) + _nbytes((tm, k), F32) + _nbytes((tm, tn), F32))
    return pl.pallas_call(
        _norm_matmul_res_kernel,
        grid=(t // tm, n // tn),
        in_specs=[pl.BlockSpec((tm, k), lambda i, j: (i, 0)),
                  pl.BlockSpec((tm, s), lambda i, j: (i, 0)),
                  pl.BlockSpec((1, k), lambda i, j: (0, 0)),
                  pl.BlockSpec((k, tn), lambda i, j: (0, j)),
                  pl.BlockSpec((tm, tn), lambda i, j: (i, j))],
        out_specs=pl.BlockSpec((tm, tn), lambda i, j: (i, j)),
        out_shape=jax.ShapeDtypeStruct((t, n), F32),
        scratch_shapes=[pltpu.VMEM((tm, k), BF16)],
        compiler_params=_params(("parallel", "arbitrary"), vmem),
        name="norm_matmul_res",
    )(yg, ssq, g.reshape(1, k), w, res)


def _conv_silu_kernel(u_ref, w_ref, b_ref, o_ref, pad_ref, *, rows):
    l, c = u_ref.shape[1], u_ref.shape[2]
    pad_ref[0:CONV_PAD_ROWS, :] = jnp.zeros((CONV_PAD_ROWS, c), F32)
    pad_ref[l + CONV_PAD_ROWS:l + 2 * CONV_PAD_ROWS, :] = jnp.zeros((CONV_PAD_ROWS, c), F32)
    pad_ref[CONV_PAD_ROWS:l + CONV_PAD_ROWS, :] = u_ref[0]
    half = SSD_CONV_W // 2

    def body(r, carry):
        r0 = pl.multiple_of(r * rows, rows)
        win = pad_ref[pl.ds(r0, rows + 2 * CONV_PAD_ROWS), :]
        acc = jnp.broadcast_to(b_ref[...], (rows, c))
        for k in range(SSD_CONV_W):
            lo = CONV_PAD_ROWS + k - half
            acc = acc + win[lo:lo + rows, :] * w_ref[k:k + 1, :]
        o_ref[0, pl.ds(r0, rows), :] = _silu(acc)
        return carry

    lax.fori_loop(0, l // rows, body, 0)


def _conv_silu(zxbc, conv_w, conv_b, *, batch, seq, tc, rows):
    cdim = conv_w.shape[1]
    col0 = D_INNER // tc
    vmem = 4 * _nbytes((seq, tc), F32) + _nbytes((seq + 2 * CONV_PAD_ROWS, tc), F32) \
        + 8 * _nbytes((rows, tc), F32)
    return pl.pallas_call(
        functools.partial(_conv_silu_kernel, rows=rows),
        grid=(batch, cdim // tc),
        in_specs=[pl.BlockSpec((1, seq, tc), lambda b, j: (b, 0, col0 + j)),
                  pl.BlockSpec((SSD_CONV_W, tc), lambda b, j: (0, j)),
                  pl.BlockSpec((1, tc), lambda b, j: (0, j))],
        out_specs=pl.BlockSpec((1, seq, tc), lambda b, j: (b, 0, j)),
        out_shape=jax.ShapeDtypeStruct((batch, seq, cdim), F32),
        scratch_shapes=[pltpu.VMEM((seq + 2 * CONV_PAD_ROWS, tc), F32)],
        compiler_params=_params(("parallel", "parallel"), vmem),
        name="conv_silu",
    )(zxbc, conv_w, conv_b.reshape(1, cdim))


def _split_dot(v, m01, dims):
    hi = v.astype(BF16)
    r1 = v - hi.astype(F32)
    mid = r1.astype(BF16)
    lo = (r1 - mid.astype(F32)).astype(BF16)
    if dims == "nn":
        dn = (((1,), (0,)), ((), ()))
        args = lambda p: (p, m01)
    else:
        dn = (((1,), (1,)), ((), ()))
        args = lambda p: (m01, p)
    out = lax.dot_general(*args(hi), dn, preferred_element_type=F32)
    out = out + lax.dot_general(*args(mid), dn, preferred_element_type=F32)
    return out + lax.dot_general(*args(lo), dn, preferred_element_type=F32)


def _softplus(v):
    return jnp.maximum(v, 0.0) + jnp.log1p(jnp.exp(-jnp.abs(v)))


def _ssd_direction(reverse, x_ref, b_ref, c_ref, dt_ref, bias_ref, alog_ref, h_ref, emit):
    q = SSD_CHUNK
    nc = x_ref.shape[1] // q
    hg, p, gd = SSD_HEADS_PER_GROUP, SSD_HEAD_DIM, SSD_GROUP_DIM
    row = lax.broadcasted_iota(jnp.int32, (q, q), 0)
    col = lax.broadcasted_iota(jnp.int32, (q, q), 1)
    if reverse:
        cum01 = (row >= col).astype(BF16)
        valid = row <= col
        last_row = 0
    else:
        cum01 = (row <= col).astype(BF16)
        valid = row >= col
        last_row = q - 1
    eye = (row == col).astype(BF16)
    er = lax.broadcasted_iota(jnp.int32, (2 * hg, 2 * gd), 0)
    ec = lax.broadcasted_iota(jnp.int32, (2 * hg, 2 * gd), 1)
    assert p & (p - 1) == 0
    expand = (lax.shift_right_logical(ec, p.bit_length() - 1) == er).astype(BF16)
    neg_a = -jnp.exp(alog_ref[...])
    bias = bias_ref[...]
    h_ref[...] = jnp.zeros(h_ref.shape, F32)

    def body(step, carry):
        chunk = (nc - 1 - step) if reverse else step
        r0 = pl.multiple_of(chunk * q, q)
        dt = _softplus(dt_ref[:, pl.ds(r0, q)] + bias)
        cs_row = _split_dot(dt * neg_a, cum01, "nn")
        cols = _split_dot(jnp.concatenate([dt, cs_row], axis=0), eye, "nt")
        lanes = _split_dot(cols, expand, "nn")
        dt_exp = lanes[:, :gd]
        cs_exp = lanes[:, gd:]
        cs_last = cs_exp[last_row:last_row + 1, :]

        x_c = x_ref[0, pl.ds(r0, q), :]
        b_c = b_ref[0, pl.ds(r0, q), :]
        c_bf = c_ref[0, pl.ds(r0, q), :].astype(BF16)
        cb = lax.dot_general(c_bf, b_c.astype(BF16), (((1,), (1,)), ((), ())),
                             preferred_element_type=F32)
        xdt = x_c * dt_exp
        xdt_bf = xdt.astype(BF16)

        h_in = h_ref[...]
        y_off = jnp.dot(c_bf, h_in.astype(BF16), preferred_element_type=F32) * jnp.exp(cs_exp)
        wx = (xdt * jnp.exp(cs_last - cs_exp)).astype(BF16)
        h_ref[...] = h_in * jnp.exp(cs_last) + jnp.dot(b_c.T.astype(BF16), wx,
                                                       preferred_element_type=F32)
        y_heads = []
        for h in range(hg):
            seg = cols[:, hg + h:hg + h + 1] - cs_row[h:h + 1, :]
            lmat = jnp.exp(jnp.where(valid, seg, -jnp.inf))
            m = (cb * lmat).astype(BF16)
            y_heads.append(jnp.dot(m, xdt_bf[:, h * p:(h + 1) * p], preferred_element_type=F32))
        emit(r0, x_c, y_off + jnp.concatenate(y_heads, axis=1))
        return carry

    lax.fori_loop(0, nc, body, 0)


def _ssd_scan_kernel(x_ref, b_ref, c_ref, z_ref, dtf_ref, dtb_ref, biasf_ref, biasb_ref,
                     alogf_ref, alogb_ref, dskip_ref, yg_ref, ssq_ref, y_acc, h_ref):
    q = SSD_CHUNK

    def emit_fwd(r0, x_c, y):
        y_acc[pl.ds(r0, q), :] = y + x_c * dskip_ref[...]

    def emit_bwd(r0, x_c, y):
        yg = (y_acc[pl.ds(r0, q), :] + y) * _silu(z_ref[0, pl.ds(r0, q), :])
        yg_ref[0, pl.ds(r0, q), :] = yg
        sq = yg * yg
        folded = sq[:, 0:V7X_LANES]
        for k in range(1, SSD_GROUP_DIM // V7X_LANES):
            folded = folded + sq[:, k * V7X_LANES:(k + 1) * V7X_LANES]
        ssq_ref[0, pl.ds(r0, q), :] = folded

    _ssd_direction(False, x_ref, b_ref, c_ref, dtf_ref, biasf_ref, alogf_ref, h_ref, emit_fwd)
    _ssd_direction(True, x_ref, b_ref, c_ref, dtb_ref, biasb_ref, alogb_ref, h_ref, emit_bwd)


def _ssd_scan(xbc, zxbc, dt_t, dt_bias, a_log, d_skip, *, batch, seq):
    g, hg, gd, n = SSD_GROUPS, SSD_HEADS_PER_GROUP, SSD_GROUP_DIM, SSD_STATE
    bcol = D_INNER // n
    ccol = (D_INNER + SSD_BC_DIM) // n
    col = lambda v: v.reshape(2 * SSD_HEADS, 1)
    dskip_lanes = jnp.repeat(d_skip, SSD_HEAD_DIM).reshape(1, D_INNER)
    per_dir = lambda d: pl.BlockSpec((hg, 1), lambda b, j: (d * g + j, 0))
    vmem = (2 * (3 * _nbytes((seq, gd), F32) + 3 * _nbytes((seq, n), F32) + 2 * _nbytes((hg, seq), F32))
            + _nbytes((seq, gd), F32) + _nbytes((n, gd), F32) + 24 * _nbytes((SSD_CHUNK, gd), F32))
    return pl.pallas_call(
        _ssd_scan_kernel,
        grid=(batch, g),
        in_specs=[pl.BlockSpec((1, seq, gd), lambda b, j: (b, 0, j)),
                  pl.BlockSpec((1, seq, n), lambda b, j: (b, 0, bcol + j)),
                  pl.BlockSpec((1, seq, n), lambda b, j: (b, 0, ccol + j)),
                  pl.BlockSpec((1, seq, gd), lambda b, j: (b, 0, j)),
                  pl.BlockSpec((hg, seq), lambda b, j: (j, b)),
                  pl.BlockSpec((hg, seq), lambda b, j: (g + j, b)),
                  per_dir(0), per_dir(1), per_dir(0), per_dir(1),
                  pl.BlockSpec((1, gd), lambda b, j: (0, j))],
        out_specs=[pl.BlockSpec((1, seq, gd), lambda b, j: (b, 0, j)),
                   pl.BlockSpec((1, seq, V7X_LANES), lambda b, j: (b, 0, j))],
        out_shape=[jax.ShapeDtypeStruct((batch, seq, D_INNER), F32),
                   jax.ShapeDtypeStruct((batch, seq, g * V7X_LANES), F32)],
        scratch_shapes=[pltpu.VMEM((seq, gd), F32), pltpu.VMEM((n, gd), F32)],
        compiler_params=_params(("parallel", "parallel"), vmem),
        name="ssd_scan",
    )(xbc, xbc, xbc, zxbc, dt_t, dt_t, col(dt_bias), col(dt_bias), col(a_log), col(a_log),
      dskip_lanes)


def _ssd_mixer(x, norm_g, w_in, conv_w, conv_b, dt_bias, a_log, d_skip, ssd_norm, w_out, *,
               batch, seq):
    w_zxbc = w_in[:, :SSD_ZXBC_DIM].astype(BF16)
    w_dt_t = w_in[:, SSD_ZXBC_DIM:].T.astype(BF16)
    zxbc = _rms_matmul(x, norm_g, w_zxbc, tm=1024, tn=1024, out_dtype=F32)
    dt_t = _rms_matmul_t(x, norm_g, w_dt_t, tm=1024)
    zxbc3 = zxbc.reshape(batch, seq, SSD_ZXBC_DIM)
    xbc = _conv_silu(zxbc3, conv_w, conv_b, batch=batch, seq=seq, tc=512, rows=64)
    yg, ssq = _ssd_scan(xbc, zxbc3, dt_t, dt_bias, a_log, d_skip, batch=batch, seq=seq)
    return _norm_matmul_res(yg.reshape(batch * seq, D_INNER), ssq.reshape(batch * seq, -1),
                            ssd_norm, w_out.astype(BF16), x, tm=512, tn=512)


def _pool_kernel(u_ref, wg_ref, sc_ref, o_ref, pad_ref, *, rows):
    l, c = u_ref.shape[1], u_ref.shape[2]
    pad_ref[0:CONV_PAD_ROWS, :] = jnp.zeros((CONV_PAD_ROWS, c), F32)
    pad_ref[l + CONV_PAD_ROWS:l + 2 * CONV_PAD_ROWS, :] = jnp.zeros((CONV_PAD_ROWS, c), F32)
    pad_ref[CONV_PAD_ROWS:l + CONV_PAD_ROWS, :] = u_ref[0]

    for gi, win in enumerate(POOL_WINDOWS):
        half = win // 2

        @pl.when(pl.program_id(1) == gi)
        def _(half=half):
            def body(r, carry):
                r0 = pl.multiple_of(r * rows, rows)
                t = r0 + lax.broadcasted_iota(jnp.int32, (rows, 1), 0)
                cnt = jnp.minimum(t + half, l) - jnp.maximum(t - half, 0)
                inv = 1.0 / cnt.astype(F32)
                win = pad_ref[pl.ds(r0, rows + 2 * CONV_PAD_ROWS), :]
                s = win[CONV_PAD_ROWS - half:CONV_PAD_ROWS - half + rows, :]
                for k in range(1 - half, half):
                    s = s + win[CONV_PAD_ROWS + k:CONV_PAD_ROWS + k + rows, :]
                mix = (s * inv - win[CONV_PAD_ROWS:CONV_PAD_ROWS + rows, :]).astype(BF16)
                v = jnp.dot(mix, wg_ref[0], preferred_element_type=F32) * sc_ref[...]
                o_ref[0, pl.ds(r0, rows), :] = v.astype(o_ref.dtype)
                return carry

            lax.fori_loop(0, l // rows, body, 0)


def _pool(u, w_group, scale, *, batch, seq, rows):
    ng, gd = len(POOL_WINDOWS), POOL_GROUP_DIM
    vmem = (2 * (_nbytes((seq, gd), F32) + _nbytes((gd, gd), BF16) + _nbytes((seq, gd), BF16))
            + _nbytes((seq + 2 * CONV_PAD_ROWS, gd), F32) + 8 * _nbytes((rows, gd), F32))
    return pl.pallas_call(
        functools.partial(_pool_kernel, rows=rows),
        grid=(batch, ng),
        in_specs=[pl.BlockSpec((1, seq, gd), lambda b, j: (b, 0, j)),
                  pl.BlockSpec((1, gd, gd), lambda b, j: (j, 0, 0)),
                  pl.BlockSpec((1, gd), lambda b, j: (0, j))],
        out_specs=pl.BlockSpec((1, seq, gd), lambda b, j: (b, 0, j)),
        out_shape=jax.ShapeDtypeStruct((batch, seq, ng * gd), BF16),
        scratch_shapes=[pltpu.VMEM((seq + 2 * CONV_PAD_ROWS, gd), F32)],
        compiler_params=_params(("parallel", "parallel"), vmem),
        name="pool",
    )(u, w_group, scale.reshape(1, ng * gd))


def _pool_mixer(x, norm_g, w_in, w_group, scale, w_out, *, batch, seq):
    u = _rms_matmul(x, norm_g, w_in.astype(BF16), tm=1024, tn=1024, out_dtype=F32)
    v = _pool(u.reshape(batch, seq, D_MODEL), w_group.astype(BF16), scale,
              batch=batch, seq=seq, rows=128)
    return _matmul_res(v.reshape(batch * seq, D_MODEL), w_out.astype(BF16), x, tm=1024, tn=512)


def _rmsnorm_kernel(x_ref, g_ref, o_ref):
    o_ref[...] = _rmsnorm_rows(x_ref[...], g_ref[...])


def _rmsnorm(x, g, *, tm):
    t, d = x.shape
    return pl.pallas_call(
        _rmsnorm_kernel,
        grid=(t // tm,),
        in_specs=[pl.BlockSpec((tm, d), lambda i: (i, 0)), pl.BlockSpec((1, d), lambda i: (0, 0))],
        out_specs=pl.BlockSpec((tm, d), lambda i: (i, 0)),
        out_shape=jax.ShapeDtypeStruct((t, d), F32),
        compiler_params=_params(("parallel",), 6 * _nbytes((tm, d), F32)),
        name="final_rmsnorm",
    )(x, g.reshape(1, d))


def kernel(x, ffn_norm, ffn_w_gate, ffn_w_up, ffn_w_down, mix_norm, ssd_w_in, ssd_conv_w,
           ssd_conv_b, ssd_dt_bias, ssd_a_log, ssd_d, ssd_norm, ssd_w_out, pool_w_in, pool_w_group,
           pool_scale, pool_w_out, final_norm):
    batch, seq, d = x.shape
    depth = ffn_norm.shape[0]
    h = x.reshape(batch * seq, d)

    def ffn(h, i, s):
        return _ffn(h, ffn_norm[i, s], ffn_w_gate[i, s].astype(BF16), ffn_w_up[i, s].astype(BF16),
                    ffn_w_down[i, s].astype(BF16), tm=512, tf=512)

    for i in range(depth):
        h = ffn(h, i, 0)
        j = i // 2
        if i % 2 == 0:
            h = _ssd_mixer(h, mix_norm[i], ssd_w_in[j], ssd_conv_w[j], ssd_conv_b[j], ssd_dt_bias[j],
                           ssd_a_log[j], ssd_d[j], ssd_norm[j], ssd_w_out[j], batch=batch, seq=seq)
        else:
            h = _pool_mixer(h, mix_norm[i], pool_w_in[j], pool_w_group[j], pool_scale[j],
                            pool_w_out[j], batch=batch, seq=seq)
        h = ffn(h, i, 1)
    return _rmsnorm(h, final_norm, tm=512).reshape(batch, seq, d)
```

```python
import functools

import jax
import jax.numpy as jnp
from jax import lax
from jax.experimental import pallas as pl
from jax.experimental.pallas import tpu as pltpu

F32 = jnp.float32
BF16 = jnp.bfloat16

EPS = 1e-6
D_MODEL = 2048
D_FF = 5632
D_INNER = 2 * D_MODEL
SSD_HEAD_DIM = 64
SSD_HEADS = D_INNER // SSD_HEAD_DIM
SSD_GROUPS = 8
SSD_HEADS_PER_GROUP = SSD_HEADS // SSD_GROUPS
SSD_GROUP_DIM = SSD_HEADS_PER_GROUP * SSD_HEAD_DIM
SSD_STATE = 128
SSD_CONV_W = 5
SSD_CHUNK = 128
SSD_BC_DIM = SSD_GROUPS * SSD_STATE
SSD_ZXBC_DIM = 2 * D_INNER + 2 * SSD_BC_DIM
POOL_WINDOWS = (2, 4, 8, 16)
POOL_GROUP_DIM = D_MODEL // len(POOL_WINDOWS)

V7X_VMEM_BYTES = 64 * 1024 * 1024
V7X_LANES = 128
V7X_SUBLANES = 8
CONV_PAD_ROWS = V7X_SUBLANES
NORM_ROW_CHUNK = 128


def _params(semantics, vmem_bytes):
    return pltpu.CompilerParams(dimension_semantics=semantics,
                                vmem_limit_bytes=min(int(vmem_bytes), V7X_VMEM_BYTES))


def _nbytes(shape, dtype):
    n = 1
    for s in shape:
        n *= s
    return n * jnp.dtype(dtype).itemsize


def _rmsnorm_rows(x, g):
    ms = jnp.mean(x * x, axis=-1, keepdims=True)
    return x * lax.rsqrt(ms + EPS) * g


def _silu(v):
    return v * jax.nn.sigmoid(v)


def _rms_matmul_kernel(x_ref, g_ref, w_ref, o_ref, xn_ref):
    @pl.when(pl.program_id(1) == 0)
    def _():
        xn_ref[...] = _rmsnorm_rows(x_ref[...], g_ref[...]).astype(BF16)

    o_ref[...] = jnp.dot(xn_ref[...], w_ref[...], preferred_element_type=F32).astype(o_ref.dtype)


def _rms_matmul(x, g, w, *, tm, tn, out_dtype):
    t, d = x.shape
    n = w.shape[1]
    vmem = (2 * (_nbytes((tm, d), F32) + _nbytes((d, tn), BF16) + _nbytes((tm, tn), out_dtype))
            + _nbytes((tm, d), BF16) + _nbytes((tm, d), F32) + _nbytes((tm, tn), F32))
    return pl.pallas_call(
        _rms_matmul_kernel,
        grid=(t // tm, n // tn),
        in_specs=[pl.BlockSpec((tm, d), lambda i, j: (i, 0)),
                  pl.BlockSpec((1, d), lambda i, j: (0, 0)),
                  pl.BlockSpec((d, tn), lambda i, j: (0, j))],
        out_specs=pl.BlockSpec((tm, tn), lambda i, j: (i, j)),
        out_shape=jax.ShapeDtypeStruct((t, n), out_dtype),
        scratch_shapes=[pltpu.VMEM((tm, d), BF16)],
        compiler_params=_params(("parallel", "arbitrary"), vmem),
        name="rms_matmul",
    )(x, g.reshape(1, d), w)


def _rms_matmul_t_kernel(x_ref, g_ref, wt_ref, o_ref):
    xn = _rmsnorm_rows(x_ref[...], g_ref[...]).astype(BF16)
    o_ref[...] = lax.dot_general(wt_ref[...], xn, (((1,), (1,)), ((), ())),
                                 preferred_element_type=F32)


def _rms_matmul_t(x, g, wt, *, tm):
    t, d = x.shape
    n = wt.shape[0]
    vmem = (2 * (_nbytes((tm, d), F32) + _nbytes((n, d), BF16) + _nbytes((n, tm), F32))
            + 2 * _nbytes((tm, d), F32))
    return pl.pallas_call(
        _rms_matmul_t_kernel,
        grid=(t // tm,),
        in_specs=[pl.BlockSpec((tm, d), lambda i: (i, 0)),
                  pl.BlockSpec((1, d), lambda i: (0, 0)),
                  pl.BlockSpec((n, d), lambda i: (0, 0))],
        out_specs=pl.BlockSpec((n, tm), lambda i: (0, i)),
        out_shape=jax.ShapeDtypeStruct((n, t), F32),
        compiler_params=_params(("parallel",), vmem),
        name="rms_matmul_t",
    )(x, g.reshape(1, d), wt)


def _ffn_kernel(x_ref, g_ref, wg_ref, wu_ref, wd_ref, o_ref, xn_ref):
    @pl.when(pl.program_id(1) == 0)
    def _():
        x = x_ref[...]
        xn_ref[...] = _rmsnorm_rows(x, g_ref[...]).astype(BF16)
        o_ref[...] = x

    xn = xn_ref[...]
    gate = jnp.dot(xn, wg_ref[...], preferred_element_type=F32)
    up = jnp.dot(xn, wu_ref[...], preferred_element_type=F32)
    hidden = (_silu(gate) * up * 0.5).astype(BF16)
    o_ref[...] += jnp.dot(hidden, wd_ref[...], preferred_element_type=F32)


def _ffn(x, g, wg, wu, wd, *, tm, tf):
    t, d = x.shape
    f = wg.shape[1]
    vmem = (2 * (2 * _nbytes((tm, d), F32) + 2 * _nbytes((d, tf), BF16) + _nbytes((tf, d), BF16))
            + _nbytes((tm, d), BF16) + 4 * _nbytes((tm, tf), F32))
    return pl.pallas_call(
        _ffn_kernel,
        grid=(t // tm, f // tf),
        in_specs=[pl.BlockSpec((tm, d), lambda i, j: (i, 0)),
                  pl.BlockSpec((1, d), lambda i, j: (0, 0)),
                  pl.BlockSpec((d, tf), lambda i, j: (0, j)),
                  pl.BlockSpec((d, tf), lambda i, j: (0, j)),
                  pl.BlockSpec((tf, d), lambda i, j: (j, 0))],
        out_specs=pl.BlockSpec((tm, d), lambda i, j: (i, 0)),
        out_shape=jax.ShapeDtypeStruct((t, d), F32),
        scratch_shapes=[pltpu.VMEM((tm, d), BF16)],
        compiler_params=_params(("parallel", "arbitrary"), vmem),
        name="ffn",
    )(x, g.reshape(1, d), wg, wu, wd)


def _matmul_res_kernel(a_ref, w_ref, r_ref, o_ref):
    o_ref[...] = r_ref[...] + jnp.dot(a_ref[...], w_ref[...], preferred_element_type=F32)


def _matmul_res(a, w, res, *, tm, tn):
    t, k = a.shape
    n = w.shape[1]
    vmem = 2 * (_nbytes((tm, k), a.dtype) + _nbytes((k, tn), BF16) + 2 * _nbytes((tm, tn), F32)) \
        + _nbytes((tm, tn), F32)
    return pl.pallas_call(
        _matmul_res_kernel,
        grid=(t // tm, n // tn),
        in_specs=[pl.BlockSpec((tm, k), lambda i, j: (i, 0)),
                  pl.BlockSpec((k, tn), lambda i, j: (0, j)),
                  pl.BlockSpec((tm, tn), lambda i, j: (i, j))],
        out_specs=pl.BlockSpec((tm, tn), lambda i, j: (i, j)),
        out_shape=jax.ShapeDtypeStruct((t, n), F32),
        compiler_params=_params(("parallel", "arbitrary"), vmem),
        name="matmul_res",
    )(a, w, res)


def _norm_matmul_res_kernel(yg_ref, ssq_ref, g_ref, w_ref, r_ref, o_ref, an_ref):
    @pl.when(pl.program_id(1) == 0)
    def _():
        rc = NORM_ROW_CHUNK

        def body(r, carry):
            rows = pl.ds(pl.multiple_of(r * rc, rc), rc)
            ms = jnp.sum(ssq_ref[rows, :], axis=-1, keepdims=True) * (1.0 / yg_ref.shape[-1])
            an_ref[rows, :] = (yg_ref[rows, :].astype(F32) * lax.rsqrt(ms + EPS)
                               * g_ref[...]).astype(BF16)
            return carry

        lax.fori_loop(0, yg_ref.shape[0] // rc, body, 0)

    o_ref[...] = r_ref[...] + jnp.dot(an_ref[...], w_ref[...], preferred_element_type=F32)


def _norm_matmul_res(yg, ssq, g, w, res, *, tm, tn):
    t, k = yg.shape
    n = w.shape[1]
    s = ssq.shape[1]
    vmem = (2 * (_nbytes((tm, k), yg.dtype) + _nbytes((tm, s), F32) + _nbytes((k, tn), BF16)
                 + 2 * _nbytes((tm, tn), F32))
            + _nbytes((tm, k), BF16) + 4 * _nbytes((tm, tn), F32))
    return pl.pallas_call(
        _norm_matmul_res_kernel,
        grid=(t // tm, n // tn),
        in_specs=[pl.BlockSpec((tm, k), lambda i, j: (i, 0)),
                  pl.BlockSpec((tm, s), lambda i, j: (i, 0)),
                  pl.BlockSpec((1, k), lambda i, j: (0, 0)),
                  pl.BlockSpec((k, tn), lambda i, j: (0, j)),
                  pl.BlockSpec((tm, tn), lambda i, j: (i, j))],
        out_specs=pl.BlockSpec((tm, tn), lambda i, j: (i, j)),
        out_shape=jax.ShapeDtypeStruct((t, n), F32),
        scratch_shapes=[pltpu.VMEM((tm, k), BF16)],
        compiler_params=_params(("parallel", "arbitrary"), vmem),
        name="norm_matmul_res",
    )(yg, ssq, g.reshape(1, k), w, res)


def _conv_silu_kernel(u_ref, w_ref, b_ref, o_ref, pad_ref, *, rows):
    l, c = u_ref.shape[1], u_ref.shape[2]
    pad_ref[0:CONV_PAD_ROWS, :] = jnp.zeros((CONV_PAD_ROWS, c), F32)
    pad_ref[l + CONV_PAD_ROWS:l + 2 * CONV_PAD_ROWS, :] = jnp.zeros((CONV_PAD_ROWS, c), F32)
    pad_ref[CONV_PAD_ROWS:l + CONV_PAD_ROWS, :] = u_ref[0].astype(F32)
    half = SSD_CONV_W // 2

    def body(r, carry):
        r0 = pl.multiple_of(r * rows, rows)
        win = pad_ref[pl.ds(r0, rows + 2 * CONV_PAD_ROWS), :]
        acc = jnp.broadcast_to(b_ref[...], (rows, c))
        for k in range(SSD_CONV_W):
            lo = CONV_PAD_ROWS + k - half
            acc = acc + win[lo:lo + rows, :] * w_ref[k:k + 1, :]
        o_ref[0, pl.ds(r0, rows), :] = _silu(acc)
        return carry

    lax.fori_loop(0, l // rows, body, 0)


def _conv_silu(zxbc, conv_w, conv_b, *, batch, seq, tc, rows):
    cdim = conv_w.shape[1]
    col0 = D_INNER // tc
    vmem = 4 * _nbytes((seq, tc), F32) + _nbytes((seq + 2 * CONV_PAD_ROWS, tc), F32) \
        + 8 * _nbytes((rows, tc), F32)
    return pl.pallas_call(
        functools.partial(_conv_silu_kernel, rows=rows),
        grid=(batch, cdim // tc),
        in_specs=[pl.BlockSpec((1, seq, tc), lambda b, j: (b, 0, col0 + j)),
                  pl.BlockSpec((SSD_CONV_W, tc), lambda b, j: (0, j)),
                  pl.BlockSpec((1, tc), lambda b, j: (0, j))],
        out_specs=pl.BlockSpec((1, seq, tc), lambda b, j: (b, 0, j)),
        out_shape=jax.ShapeDtypeStruct((batch, seq, cdim), F32),
        scratch_shapes=[pltpu.VMEM((seq + 2 * CONV_PAD_ROWS, tc), F32)],
        compiler_params=_params(("parallel", "parallel"), vmem),
        name="conv_silu",
    )(zxbc, conv_w, conv_b.reshape(1, cdim))


def _split_dot(v, m01, dims):
    hi = v.astype(BF16)
    r1 = v - hi.astype(F32)
    mid = r1.astype(BF16)
    lo = (r1 - mid.astype(F32)).astype(BF16)
    if dims == "nn":
        dn = (((1,), (0,)), ((), ()))
        args = lambda p: (p, m01)
    else:
        dn = (((1,), (1,)), ((), ()))
        args = lambda p: (m01, p)
    out = lax.dot_general(*args(hi), dn, preferred_element_type=F32)
    out = out + lax.dot_general(*args(mid), dn, preferred_element_type=F32)
    return out + lax.dot_general(*args(lo), dn, preferred_element_type=F32)


def _softplus(v):
    return jnp.maximum(v, 0.0) + jnp.log1p(jnp.exp(-jnp.abs(v)))


def _ssd_direction(reverse, x_ref, b_ref, c_ref, dt_ref, bias_ref, alog_ref, h_ref, emit):
    q = SSD_CHUNK
    nc = x_ref.shape[1] // q
    hg, p, gd = SSD_HEADS_PER_GROUP, SSD_HEAD_DIM, SSD_GROUP_DIM
    row = lax.broadcasted_iota(jnp.int32, (q, q), 0)
    col = lax.broadcasted_iota(jnp.int32, (q, q), 1)
    if reverse:
        cum01 = (row >= col).astype(BF16)
        valid = row <= col
        last_row = 0
    else:
        cum01 = (row <= col).astype(BF16)
        valid = row >= col
        last_row = q - 1
    eye = (row == col).astype(BF16)
    er = lax.broadcasted_iota(jnp.int32, (2 * hg, 2 * gd), 0)
    ec = lax.broadcasted_iota(jnp.int32, (2 * hg, 2 * gd), 1)
    assert p & (p - 1) == 0
    expand = (lax.shift_right_logical(ec, p.bit_length() - 1) == er).astype(BF16)
    neg_a = -jnp.exp(alog_ref[...])
    bias = bias_ref[...]
    h_ref[...] = jnp.zeros(h_ref.shape, F32)

    def body(step, carry):
        chunk = (nc - 1 - step) if reverse else step
        r0 = pl.multiple_of(chunk * q, q)
        dt = _softplus(dt_ref[:, pl.ds(r0, q)] + bias)
        cs_row = _split_dot(dt * neg_a, cum01, "nn")
        cols = _split_dot(jnp.concatenate([dt, cs_row], axis=0), eye, "nt")
        lanes = _split_dot(cols, expand, "nn")
        dt_exp = lanes[:, :gd]
        cs_exp = lanes[:, gd:]
        cs_last = cs_exp[last_row:last_row + 1, :]

        x_c = x_ref[0, pl.ds(r0, q), :]
        b_c = b_ref[0, pl.ds(r0, q), :]
        c_bf = c_ref[0, pl.ds(r0, q), :].astype(BF16)
        cb = lax.dot_general(c_bf, b_c.astype(BF16), (((1,), (1,)), ((), ())),
                             preferred_element_type=F32)
        xdt = x_c * dt_exp
        xdt_bf = xdt.astype(BF16)

        h_in = h_ref[...]
        y_off = jnp.dot(c_bf, h_in.astype(BF16), preferred_element_type=F32) * jnp.exp(cs_exp)
        wx = (xdt * jnp.exp(cs_last - cs_exp)).astype(BF16)
        h_ref[...] = h_in * jnp.exp(cs_last) + jnp.dot(b_c.T.astype(BF16), wx,
                                                       preferred_element_type=F32)
        y_heads = []
        for h in range(hg):
            seg = cols[:, hg + h:hg + h + 1] - cs_row[h:h + 1, :]
            lmat = jnp.exp(jnp.where(valid, seg, -jnp.inf))
            m = (cb * lmat).astype(BF16)
            y_heads.append(jnp.dot(m, xdt_bf[:, h * p:(h + 1) * p], preferred_element_type=F32))
        emit(r0, x_c, y_off + jnp.concatenate(y_heads, axis=1))
        return carry

    lax.fori_loop(0, nc, body, 0)


def _ssd_scan_kernel(x_ref, b_ref, c_ref, z_ref, dtf_ref, dtb_ref, biasf_ref, biasb_ref,
                     alogf_ref, alogb_ref, dskip_ref, yg_ref, ssq_ref, y_acc, h_ref):
    q = SSD_CHUNK

    def emit_fwd(r0, x_c, y):
        y_acc[pl.ds(r0, q), :] = y + x_c * dskip_ref[...]

    def emit_bwd(r0, x_c, y):
        yg = (y_acc[pl.ds(r0, q), :] + y) * _silu(z_ref[0, pl.ds(r0, q), :].astype(F32))
        yg_ref[0, pl.ds(r0, q), :] = yg.astype(yg_ref.dtype)
        sq = yg * yg
        folded = sq[:, 0:V7X_LANES]
        for k in range(1, SSD_GROUP_DIM // V7X_LANES):
            folded = folded + sq[:, k * V7X_LANES:(k + 1) * V7X_LANES]
        ssq_ref[0, pl.ds(r0, q), :] = folded

    _ssd_direction(False, x_ref, b_ref, c_ref, dtf_ref, biasf_ref, alogf_ref, h_ref, emit_fwd)
    _ssd_direction(True, x_ref, b_ref, c_ref, dtb_ref, biasb_ref, alogb_ref, h_ref, emit_bwd)


def _ssd_scan(xbc, zxbc, dt_t, dt_bias, a_log, d_skip, *, batch, seq):
    g, hg, gd, n = SSD_GROUPS, SSD_HEADS_PER_GROUP, SSD_GROUP_DIM, SSD_STATE
    bcol = D_INNER // n
    ccol = (D_INNER + SSD_BC_DIM) // n
    col = lambda v: v.reshape(2 * SSD_HEADS, 1)
    dskip_lanes = jnp.repeat(d_skip, SSD_HEAD_DIM).reshape(1, D_INNER)
    per_dir = lambda d: pl.BlockSpec((hg, 1), lambda b, j: (d * g + j, 0))
    vmem = (2 * (3 * _nbytes((seq, gd), F32) + 3 * _nbytes((seq, n), F32) + 2 * _nbytes((hg, seq), F32))
            + _nbytes((seq, gd), F32) + _nbytes((n, gd), F32) + 24 * _nbytes((SSD_CHUNK, gd), F32))
    return pl.pallas_call(
        _ssd_scan_kernel,
        grid=(batch, g),
        in_specs=[pl.BlockSpec((1, seq, gd), lambda b, j: (b, 0, j)),
                  pl.BlockSpec((1, seq, n), lambda b, j: (b, 0, bcol + j)),
                  pl.BlockSpec((1, seq, n), lambda b, j: (b, 0, ccol + j)),
                  pl.BlockSpec((1, seq, gd), lambda b, j: (b, 0, j)),
                  pl.BlockSpec((hg, seq), lambda b, j: (j, b)),
                  pl.BlockSpec((hg, seq), lambda b, j: (g + j, b)),
                  per_dir(0), per_dir(1), per_dir(0), per_dir(1),
                  pl.BlockSpec((1, gd), lambda b, j: (0, j))],
        out_specs=[pl.BlockSpec((1, seq, gd), lambda b, j: (b, 0, j)),
                   pl.BlockSpec((1, seq, V7X_LANES), lambda b, j: (b, 0, j))],
        out_shape=[jax.ShapeDtypeStruct((batch, seq, D_INNER), BF16),
                   jax.ShapeDtypeStruct((batch, seq, g * V7X_LANES), F32)],
        scratch_shapes=[pltpu.VMEM((seq, gd), F32), pltpu.VMEM((n, gd), F32)],
        compiler_params=_params(("parallel", "parallel"), vmem),
        name="ssd_scan",
    )(xbc, xbc, xbc, zxbc, dt_t, dt_t, col(dt_bias), col(dt_bias), col(a_log), col(a_log),
      dskip_lanes)


def _ssd_mixer(x, norm_g, w_in, conv_w, conv_b, dt_bias, a_log, d_skip, ssd_norm, w_out, *,
               batch, seq):
    w_zxbc = w_in[:, :SSD_ZXBC_DIM].astype(BF16)
    w_dt_t = w_in[:, SSD_ZXBC_DIM:].T.astype(BF16)
    zxbc = _rms_matmul(x, norm_g, w_zxbc, tm=1024, tn=1024, out_dtype=BF16)
    dt_t = _rms_matmul_t(x, norm_g, w_dt_t, tm=1024)
    zxbc3 = zxbc.reshape(batch, seq, SSD_ZXBC_DIM)
    xbc = _conv_silu(zxbc3, conv_w, conv_b, batch=batch, seq=seq, tc=512, rows=64)
    yg, ssq = _ssd_scan(xbc, zxbc3, dt_t, dt_bias, a_log, d_skip, batch=batch, seq=seq)
    return _norm_matmul_res(yg.reshape(batch * seq, D_INNER), ssq.reshape(batch * seq, -1),
                            ssd_norm, w_out.astype(BF16), x, tm=1024, tn=512)


def _pool_kernel(u_ref, wg_ref, sc_ref, o_ref, pad_ref, *, rows):
    l, c = u_ref.shape[1], u_ref.shape[2]
    pad_ref[0:CONV_PAD_ROWS, :] = jnp.zeros((CONV_PAD_ROWS, c), F32)
    pad_ref[l + CONV_PAD_ROWS:l + 2 * CONV_PAD_ROWS, :] = jnp.zeros((CONV_PAD_ROWS, c), F32)
    pad_ref[CONV_PAD_ROWS:l + CONV_PAD_ROWS, :] = u_ref[0].astype(F32)

    for gi, win in enumerate(POOL_WINDOWS):
        half = win // 2

        @pl.when(pl.program_id(1) == gi)
        def _(half=half):
            def body(r, carry):
                r0 = pl.multiple_of(r * rows, rows)
                t = r0 + lax.broadcasted_iota(jnp.int32, (rows, 1), 0)
                cnt = jnp.minimum(t + half, l) - jnp.maximum(t - half, 0)
                inv = 1.0 / cnt.astype(F32)
                win = pad_ref[pl.ds(r0, rows + 2 * CONV_PAD_ROWS), :]
                s = win[CONV_PAD_ROWS - half:CONV_PAD_ROWS - half + rows, :]
                for k in range(1 - half, half):
                    s = s + win[CONV_PAD_ROWS + k:CONV_PAD_ROWS + k + rows, :]
                mix = (s * inv - win[CONV_PAD_ROWS:CONV_PAD_ROWS + rows, :]).astype(BF16)
                v = jnp.dot(mix, wg_ref[0], preferred_element_type=F32) * sc_ref[...]
                o_ref[0, pl.ds(r0, rows), :] = v.astype(o_ref.dtype)
                return carry

            lax.fori_loop(0, l // rows, body, 0)


def _pool(u, w_group, scale, *, batch, seq, rows):
    ng, gd = len(POOL_WINDOWS), POOL_GROUP_DIM
    vmem = (2 * (_nbytes((seq, gd), F32) + _nbytes((gd, gd), BF16) + _nbytes((seq, gd), BF16))
            + _nbytes((seq + 2 * CONV_PAD_ROWS, gd), F32) + 8 * _nbytes((rows, gd), F32))
    return pl.pallas_call(
        functools.partial(_pool_kernel, rows=rows),
        grid=(batch, ng),
        in_specs=[pl.BlockSpec((1, seq, gd), lambda b, j: (b, 0, j)),
                  pl.BlockSpec((1, gd, gd), lambda b, j: (j, 0, 0)),
                  pl.BlockSpec((1, gd), lambda b, j: (0, j))],
        out_specs=pl.BlockSpec((1, seq, gd), lambda b, j: (b, 0, j)),
        out_shape=jax.ShapeDtypeStruct((batch, seq, ng * gd), BF16),
        scratch_shapes=[pltpu.VMEM((seq + 2 * CONV_PAD_ROWS, gd), F32)],
        compiler_params=_params(("parallel", "parallel"), vmem),
        name="pool",
    )(u, w_group, scale.reshape(1, ng * gd))


def _pool_mixer(x, norm_g, w_in, w_group, scale, w_out, *, batch, seq):
    u = _rms_matmul(x, norm_g, w_in.astype(BF16), tm=1024, tn=1024, out_dtype=F32)
    v = _pool(u.reshape(batch, seq, D_MODEL), w_group.astype(BF16), scale,
              batch=batch, seq=seq, rows=128)
    return _matmul_res(v.reshape(batch * seq, D_MODEL), w_out.astype(BF16), x, tm=1024, tn=512)


def _rmsnorm_kernel(x_ref, g_ref, o_ref):
    o_ref[...] = _rmsnorm_rows(x_ref[...], g_ref[...])


def _rmsnorm(x, g, *, tm):
    t, d = x.shape
    return pl.pallas_call(
        _rmsnorm_kernel,
        grid=(t // tm,),
        in_specs=[pl.BlockSpec((tm, d), lambda i: (i, 0)), pl.BlockSpec((1, d), lambda i: (0, 0))],
        out_specs=pl.BlockSpec((tm, d), lambda i: (i, 0)),
        out_shape=jax.ShapeDtypeStruct((t, d), F32),
        compiler_params=_params(("parallel",), 6 * _nbytes((tm, d), F32)),
        name="final_rmsnorm",
    )(x, g.reshape(1, d))


def kernel(x, ffn_norm, ffn_w_gate, ffn_w_up, ffn_w_down, mix_norm, ssd_w_in, ssd_conv_w,
           ssd_conv_b, ssd_dt_bias, ssd_a_log, ssd_d, ssd_norm, ssd_w_out, pool_w_in, pool_w_group,
           pool_scale, pool_w_out, final_norm):
    batch, seq, d = x.shape
    depth = ffn_norm.shape[0]
    h = x.reshape(batch * seq, d)

    def ffn(h, i, s):
        return _ffn(h, ffn_norm[i, s], ffn_w_gate[i, s].astype(BF16), ffn_w_up[i, s].astype(BF16),
                    ffn_w_down[i, s].astype(BF16), tm=1024, tf=512)

    for i in range(depth):
        h = ffn(h, i, 0)
        j = i // 2
        if i % 2 == 0:
            h = _ssd_mixer(h, mix_norm[i], ssd_w_in[j], ssd_conv_w[j], ssd_conv_b[j], ssd_dt_bias[j],
                           ssd_a_log[j], ssd_d[j], ssd_norm[j], ssd_w_out[j], batch=batch, seq=seq)
        else:
            h = _pool_mixer(h, mix_norm[i], pool_w_in[j], pool_w_group[j], pool_scale[j],
                            pool_w_out[j], batch=batch, seq=seq)
        h = ffn(h, i, 1)
    return _rmsnorm(h, final_norm, tm=512).reshape(batch, seq, d)
```

```python
import functools

import jax
import jax.numpy as jnp
from jax import lax
from jax.experimental import pallas as pl
from jax.experimental.pallas import tpu as pltpu

F32 = jnp.float32
BF16 = jnp.bfloat16

EPS = 1e-6
D_MODEL = 2048
D_FF = 5632
D_INNER = 2 * D_MODEL
SSD_HEAD_DIM = 64
SSD_HEADS = D_INNER // SSD_HEAD_DIM
SSD_GROUPS = 8
SSD_HEADS_PER_GROUP = SSD_HEADS // SSD_GROUPS
SSD_GROUP_DIM = SSD_HEADS_PER_GROUP * SSD_HEAD_DIM
SSD_STATE = 128
SSD_CONV_W = 5
SSD_CHUNK = 128
SSD_BC_DIM = SSD_GROUPS * SSD_STATE
SSD_ZXBC_DIM = 2 * D_INNER + 2 * SSD_BC_DIM
POOL_WINDOWS = (2, 4, 8, 16)
POOL_GROUP_DIM = D_MODEL // len(POOL_WINDOWS)

V7X_VMEM_BYTES = 64 * 1024 * 1024
V7X_LANES = 128
V7X_SUBLANES = 8
V7X_MXU_WIDTH = 256
CONV_PAD_ROWS = V7X_SUBLANES
NORM_ROW_CHUNK = 128


def _params(semantics, vmem_bytes):
    return pltpu.CompilerParams(dimension_semantics=semantics,
                                vmem_limit_bytes=min(int(vmem_bytes), V7X_VMEM_BYTES))


def _nbytes(shape, dtype):
    n = 1
    for s in shape:
        n *= s
    return n * jnp.dtype(dtype).itemsize


def _rmsnorm_rows(x, g):
    ms = jnp.mean(x * x, axis=-1, keepdims=True)
    return x * lax.rsqrt(ms + EPS) * g


def _silu(v):
    return v * jax.nn.sigmoid(v)


def _rms_matmul_kernel(x_ref, g_ref, w_ref, o_ref, xn_ref):
    @pl.when(pl.program_id(1) == 0)
    def _():
        xn_ref[...] = _rmsnorm_rows(x_ref[...], g_ref[...]).astype(BF16)

    o_ref[...] = jnp.dot(xn_ref[...], w_ref[...], preferred_element_type=F32).astype(o_ref.dtype)


def _rms_matmul(x, g, w, *, tm, tn, out_dtype):
    t, d = x.shape
    n = w.shape[1]
    vmem = (2 * (_nbytes((tm, d), F32) + _nbytes((d, tn), BF16) + _nbytes((tm, tn), out_dtype))
            + _nbytes((tm, d), BF16) + _nbytes((tm, d), F32) + _nbytes((tm, tn), F32))
    return pl.pallas_call(
        _rms_matmul_kernel,
        grid=(t // tm, n // tn),
        in_specs=[pl.BlockSpec((tm, d), lambda i, j: (i, 0)),
                  pl.BlockSpec((1, d), lambda i, j: (0, 0)),
                  pl.BlockSpec((d, tn), lambda i, j: (0, j))],
        out_specs=pl.BlockSpec((tm, tn), lambda i, j: (i, j)),
        out_shape=jax.ShapeDtypeStruct((t, n), out_dtype),
        scratch_shapes=[pltpu.VMEM((tm, d), BF16)],
        compiler_params=_params(("parallel", "arbitrary"), vmem),
        name="rms_matmul",
    )(x, g.reshape(1, d), w)


def _rms_matmul_t_kernel(x_ref, g_ref, wt_ref, o_ref):
    xn = _rmsnorm_rows(x_ref[...], g_ref[...]).astype(BF16)
    o_ref[...] = lax.dot_general(wt_ref[...], xn, (((1,), (1,)), ((), ())),
                                 preferred_element_type=F32)


def _rms_matmul_t(x, g, wt, *, tm):
    t, d = x.shape
    n = wt.shape[0]
    vmem = (2 * (_nbytes((tm, d), F32) + _nbytes((n, d), BF16) + _nbytes((n, tm), F32))
            + 2 * _nbytes((tm, d), F32))
    return pl.pallas_call(
        _rms_matmul_t_kernel,
        grid=(t // tm,),
        in_specs=[pl.BlockSpec((tm, d), lambda i: (i, 0)),
                  pl.BlockSpec((1, d), lambda i: (0, 0)),
                  pl.BlockSpec((n, d), lambda i: (0, 0))],
        out_specs=pl.BlockSpec((n, tm), lambda i: (0, i)),
        out_shape=jax.ShapeDtypeStruct((n, t), F32),
        compiler_params=_params(("parallel",), vmem),
        name="rms_matmul_t",
    )(x, g.reshape(1, d), wt)


def _ffn_kernel(x_ref, g_ref, wg_ref, wu_ref, wd_ref, o_ref, xn_ref):
    @pl.when(pl.program_id(1) == 0)
    def _():
        x = x_ref[...]
        xn_ref[...] = _rmsnorm_rows(x, g_ref[...]).astype(BF16)
        o_ref[...] = x

    xn = xn_ref[...]
    gate = jnp.dot(xn, wg_ref[...], preferred_element_type=F32)
    up = jnp.dot(xn, wu_ref[...], preferred_element_type=F32)
    hidden = (_silu(gate) * up * 0.5).astype(BF16)
    o_ref[...] += jnp.dot(hidden, wd_ref[...], preferred_element_type=F32)


def _ffn(x, g, wg, wu, wd, *, tm, tf):
    t, d = x.shape
    f = wg.shape[1]
    vmem = (2 * (2 * _nbytes((tm, d), F32) + 2 * _nbytes((d, tf), BF16) + _nbytes((tf, d), BF16))
            + _nbytes((tm, d), BF16) + 4 * _nbytes((tm, tf), F32))
    return pl.pallas_call(
        _ffn_kernel,
        grid=(t // tm, f // tf),
        in_specs=[pl.BlockSpec((tm, d), lambda i, j: (i, 0)),
                  pl.BlockSpec((1, d), lambda i, j: (0, 0)),
                  pl.BlockSpec((d, tf), lambda i, j: (0, j)),
                  pl.BlockSpec((d, tf), lambda i, j: (0, j)),
                  pl.BlockSpec((tf, d), lambda i, j: (j, 0))],
        out_specs=pl.BlockSpec((tm, d), lambda i, j: (i, 0)),
        out_shape=jax.ShapeDtypeStruct((t, d), F32),
        scratch_shapes=[pltpu.VMEM((tm, d), BF16)],
        compiler_params=_params(("parallel", "arbitrary"), vmem),
        name="ffn",
    )(x, g.reshape(1, d), wg, wu, wd)


def _matmul_res_kernel(a_ref, w_ref, r_ref, o_ref):
    o_ref[...] = r_ref[...] + jnp.dot(a_ref[...], w_ref[...], preferred_element_type=F32)


def _matmul_res(a, w, res, *, tm, tn):
    t, k = a.shape
    n = w.shape[1]
    vmem = 2 * (_nbytes((tm, k), a.dtype) + _nbytes((k, tn), BF16) + 2 * _nbytes((tm, tn), F32)) \
        + _nbytes((tm, tn), F32)
    return pl.pallas_call(
        _matmul_res_kernel,
        grid=(t // tm, n // tn),
        in_specs=[pl.BlockSpec((tm, k), lambda i, j: (i, 0)),
                  pl.BlockSpec((k, tn), lambda i, j: (0, j)),
                  pl.BlockSpec((tm, tn), lambda i, j: (i, j))],
        out_specs=pl.BlockSpec((tm, tn), lambda i, j: (i, j)),
        out_shape=jax.ShapeDtypeStruct((t, n), F32),
        compiler_params=_params(("parallel", "arbitrary"), vmem),
        name="matmul_res",
    )(a, w, res)


def _norm_matmul_res_kernel(yg_ref, ssq_ref, g_ref, w_ref, r_ref, o_ref, an_ref):
    @pl.when(pl.program_id(1) == 0)
    def _():
        rc = NORM_ROW_CHUNK

        def body(r, carry):
            rows = pl.ds(pl.multiple_of(r * rc, rc), rc)
            ms = jnp.sum(ssq_ref[rows, :], axis=-1, keepdims=True) * (1.0 / yg_ref.shape[-1])
            an_ref[rows, :] = (yg_ref[rows, :].astype(F32) * lax.rsqrt(ms + EPS)
                               * g_ref[...]).astype(BF16)
            return carry

        lax.fori_loop(0, yg_ref.shape[0] // rc, body, 0)

    o_ref[...] = r_ref[...] + jnp.dot(an_ref[...], w_ref[...], preferred_element_type=F32)


def _norm_matmul_res(yg, ssq, g, w, res, *, tm, tn):
    t, k = yg.shape
    n = w.shape[1]
    s = ssq.shape[1]
    vmem = (2 * (_nbytes((tm, k), yg.dtype) + _nbytes((tm, s), F32) + _nbytes((k, tn), BF16)
                 + 2 * _nbytes((tm, tn), F32))
            + _nbytes((tm, k), BF16) + 4 * _nbytes((tm, tn), F32))
    return pl.pallas_call(
        _norm_matmul_res_kernel,
        grid=(t // tm, n // tn),
        in_specs=[pl.BlockSpec((tm, k), lambda i, j: (i, 0)),
                  pl.BlockSpec((tm, s), lambda i, j: (i, 0)),
                  pl.BlockSpec((1, k), lambda i, j: (0, 0)),
                  pl.BlockSpec((k, tn), lambda i, j: (0, j)),
                  pl.BlockSpec((tm, tn), lambda i, j: (i, j))],
        out_specs=pl.BlockSpec((tm, tn), lambda i, j: (i, j)),
        out_shape=jax.ShapeDtypeStruct((t, n), F32),
        scratch_shapes=[pltpu.VMEM((tm, k), BF16)],
        compiler_params=_params(("parallel", "arbitrary"), vmem),
        name="norm_matmul_res",
    )(yg, ssq, g.reshape(1, k), w, res)


def _conv_silu_kernel(u_ref, w_ref, b_ref, o_ref, pad_ref, *, rows):
    l, c = u_ref.shape[1], u_ref.shape[2]
    pad_ref[0:CONV_PAD_ROWS, :] = jnp.zeros((CONV_PAD_ROWS, c), F32)
    pad_ref[l + CONV_PAD_ROWS:l + 2 * CONV_PAD_ROWS, :] = jnp.zeros((CONV_PAD_ROWS, c), F32)
    pad_ref[CONV_PAD_ROWS:l + CONV_PAD_ROWS, :] = u_ref[0].astype(F32)
    half = SSD_CONV_W // 2

    def body(r, carry):
        r0 = pl.multiple_of(r * rows, rows)
        win = pad_ref[pl.ds(r0, rows + 2 * CONV_PAD_ROWS), :]
        acc = jnp.broadcast_to(b_ref[...], (rows, c))
        for k in range(SSD_CONV_W):
            lo = CONV_PAD_ROWS + k - half
            acc = acc + win[lo:lo + rows, :] * w_ref[k:k + 1, :]
        o_ref[0, pl.ds(r0, rows), :] = _silu(acc)
        return carry

    lax.fori_loop(0, l // rows, body, 0)


def _conv_silu(zxbc, conv_w, conv_b, *, batch, seq, tc, rows):
    cdim = conv_w.shape[1]
    col0 = D_INNER // tc
    vmem = 4 * _nbytes((seq, tc), F32) + _nbytes((seq + 2 * CONV_PAD_ROWS, tc), F32) \
        + 8 * _nbytes((rows, tc), F32)
    return pl.pallas_call(
        functools.partial(_conv_silu_kernel, rows=rows),
        grid=(batch, cdim // tc),
        in_specs=[pl.BlockSpec((1, seq, tc), lambda b, j: (b, 0, col0 + j)),
                  pl.BlockSpec((SSD_CONV_W, tc), lambda b, j: (0, j)),
                  pl.BlockSpec((1, tc), lambda b, j: (0, j))],
        out_specs=pl.BlockSpec((1, seq, tc), lambda b, j: (b, 0, j)),
        out_shape=jax.ShapeDtypeStruct((batch, seq, cdim), F32),
        scratch_shapes=[pltpu.VMEM((seq + 2 * CONV_PAD_ROWS, tc), F32)],
        compiler_params=_params(("parallel", "parallel"), vmem),
        name="conv_silu",
    )(zxbc, conv_w, conv_b.reshape(1, cdim))


def _split_dot(v, m01):
    hi = v.astype(BF16)
    r1 = v - hi.astype(F32)
    mid = r1.astype(BF16)
    lo = (r1 - mid.astype(F32)).astype(BF16)
    out = jnp.dot(hi, m01, preferred_element_type=F32)
    out = out + jnp.dot(mid, m01, preferred_element_type=F32)
    return out + jnp.dot(lo, m01, preferred_element_type=F32)


def _softplus(v):
    return jnp.maximum(v, 0.0) + jnp.log1p(jnp.exp(-jnp.abs(v)))


def _log2(n):
    assert n & (n - 1) == 0
    return n.bit_length() - 1


def _ssd_prepare(reverse, dt_ref, bias_ref, alog_ref, r_cs, r_dt, t_cs, t_w, t_e, dec_ref):
    q, hg, p = SSD_CHUNK, SSD_HEADS_PER_GROUP, SSD_HEAD_DIM
    nc = dt_ref.shape[1] // q
    rows = nc * hg
    dt_all = _softplus(dt_ref[...] + bias_ref[...])
    dta_all = dt_all * (-jnp.exp(alog_ref[...]))
    stack = lambda v: jnp.concatenate([v[:, c * q:(c + 1) * q] for c in range(nc)], axis=0)
    dt_r = stack(dt_all)
    src = lax.broadcasted_iota(jnp.int32, (q, q), 0)
    dst = lax.broadcasted_iota(jnp.int32, (q, q), 1)
    cum01 = ((src >= dst) if reverse else (src <= dst)).astype(BF16)
    last_pos = 0 if reverse else q - 1
    cs = _split_dot(stack(dta_all), cum01)
    cs_last = cs[:, last_pos:last_pos + 1]
    cs_t = cs.T
    r_cs[...] = cs
    r_dt[...] = dt_r
    t_cs[...] = cs_t
    t_w[...] = (dt_r * jnp.exp(cs_last - cs)).T
    t_e[...] = jnp.exp(cs).T
    band_row = lax.shift_right_logical(lax.broadcasted_iota(jnp.int32, (rows, rows), 0), _log2(hg))
    band_col = lax.shift_right_logical(lax.broadcasted_iota(jnp.int32, (rows, rows), 1), _log2(hg))
    dec_sel = jnp.where(band_row == band_col, jnp.exp(cs_t[last_pos:last_pos + 1, :]), 0.0)
    head_of_row = lax.broadcasted_iota(jnp.int32, (rows, SSD_GROUP_DIM), 0) & (hg - 1)
    head_of_lane = lax.shift_right_logical(
        lax.broadcasted_iota(jnp.int32, (rows, SSD_GROUP_DIM), 1), _log2(p))
    dec_ref[...] = _split_dot(dec_sel, (head_of_row == head_of_lane).astype(BF16))


def _ssd_direction(reverse, x_ref, b_ref, c_ref, h_ref, r_cs, r_dt, t_cs, t_w, t_e, dec_ref, emit):
    q, hg, p, gd = SSD_CHUNK, SSD_HEADS_PER_GROUP, SSD_HEAD_DIM, SSD_GROUP_DIM
    nc = x_ref.shape[1] // q
    rows = nc * hg
    per_tile = V7X_MXU_WIDTH // p
    tgt = lax.broadcasted_iota(jnp.int32, (q, q), 0)
    src = lax.broadcasted_iota(jnp.int32, (q, q), 1)
    valid = (tgt <= src) if reverse else (tgt >= src)
    assert rows == V7X_LANES and 2 * p == V7X_LANES
    lane = lax.broadcasted_iota(jnp.int32, (q, V7X_LANES), 1)
    second_of_pair = lax.shift_right_logical(lane, _log2(p))
    tile_head = lax.shift_right_logical(
        lax.broadcasted_iota(jnp.int32, (q, V7X_MXU_WIDTH), 1), _log2(p))
    h_ref[...] = jnp.zeros(h_ref.shape, F32)

    def body(step, carry):
        chunk = (nc - 1 - step) if reverse else step
        r0 = pl.multiple_of(chunk * q, q)
        band = pl.ds(pl.multiple_of(chunk * hg, hg), hg)
        lane0 = chunk * hg

        def spread(t_ref):
            cols = t_ref[...]
            return jnp.concatenate(
                [jnp.take_along_axis(cols, lane0 + k + second_of_pair, axis=1)
                 for k in range(0, hg, 2)], axis=1)

        cs_all = t_cs[...]
        cs_rows = r_cs[band, :]
        dt_rows = r_dt[band, :]
        dec = dec_ref[band, :][0:1, :]

        x_c = x_ref[0, pl.ds(r0, q), :]
        x_bf = x_c.astype(BF16)
        b_c = b_ref[0, pl.ds(r0, q), :]
        c_bf = c_ref[0, pl.ds(r0, q), :].astype(BF16)
        cb = lax.dot_general(c_bf, b_c.astype(BF16), (((1,), (1,)), ((), ())),
                             preferred_element_type=F32)
        h_in = h_ref[...]
        y = jnp.dot(c_bf, h_in.astype(BF16), preferred_element_type=F32) * spread(t_e)
        wx = (x_c * spread(t_w)).astype(BF16)
        h_ref[...] = h_in * dec + jnp.dot(b_c.T.astype(BF16), wx, preferred_element_type=F32)

        y_tiles = []
        for t0 in range(0, hg, per_tile):
            ms, xs = [], []
            x_tile = x_bf[:, t0 * p:(t0 + per_tile) * p]
            for k in range(per_tile):
                h = t0 + k
                cs_tgt = jnp.take_along_axis(cs_all, jnp.broadcast_to(lane0 + h, lane.shape), axis=1)
                seg = cs_tgt - cs_rows[h:h + 1, :]
                lmat = jnp.exp(jnp.where(valid, seg, -jnp.inf))
                ms.append((cb * lmat * dt_rows[h:h + 1, :]).astype(BF16))
                xs.append(jnp.where(tile_head == k, x_tile, jnp.zeros_like(x_tile)))
            y_tiles.append(jnp.dot(jnp.concatenate(ms, axis=1), jnp.concatenate(xs, axis=0),
                                   preferred_element_type=F32))
        emit(r0, x_c, y + jnp.concatenate(y_tiles, axis=1))
        return carry

    lax.fori_loop(0, nc, body, 0, unroll=2)


def _ssd_scan_kernel(x_ref, b_ref, c_ref, z_ref, dtf_ref, dtb_ref, biasf_ref, biasb_ref,
                     alogf_ref, alogb_ref, dskip_ref, yg_ref, ssq_ref,
                     y_acc, h_ref, r_cs, r_dt, t_cs, t_w, t_e, dec_ref):
    q = SSD_CHUNK
    decay = (r_cs, r_dt, t_cs, t_w, t_e, dec_ref)

    def emit_fwd(r0, x_c, y):
        y_acc[pl.ds(r0, q), :] = y + x_c * dskip_ref[...]

    def emit_bwd(r0, x_c, y):
        yg = (y_acc[pl.ds(r0, q), :] + y) * _silu(z_ref[0, pl.ds(r0, q), :].astype(F32))
        yg_ref[0, pl.ds(r0, q), :] = yg.astype(yg_ref.dtype)
        sq = yg * yg
        folded = sq[:, 0:V7X_LANES]
        for k in range(1, SSD_GROUP_DIM // V7X_LANES):
            folded = folded + sq[:, k * V7X_LANES:(k + 1) * V7X_LANES]
        ssq_ref[0, pl.ds(r0, q), :] = folded

    _ssd_prepare(False, dtf_ref, biasf_ref, alogf_ref, *decay)
    _ssd_direction(False, x_ref, b_ref, c_ref, h_ref, *decay, emit_fwd)
    _ssd_prepare(True, dtb_ref, biasb_ref, alogb_ref, *decay)
    _ssd_direction(True, x_ref, b_ref, c_ref, h_ref, *decay, emit_bwd)


def _ssd_scan(xbc, zxbc, dt_t, dt_bias, a_log, d_skip, *, batch, seq):
    g, hg, gd, n = SSD_GROUPS, SSD_HEADS_PER_GROUP, SSD_GROUP_DIM, SSD_STATE
    bcol = D_INNER // n
    ccol = (D_INNER + SSD_BC_DIM) // n
    col = lambda v: v.reshape(2 * SSD_HEADS, 1)
    dskip_lanes = jnp.repeat(d_skip, SSD_HEAD_DIM).reshape(1, D_INNER)
    per_dir = lambda d: pl.BlockSpec((hg, 1), lambda b, j: (d * g + j, 0))
    rows = (seq // SSD_CHUNK) * hg
    vmem = (2 * (3 * _nbytes((seq, gd), F32) + 3 * _nbytes((seq, n), F32) + 2 * _nbytes((hg, seq), F32))
            + _nbytes((seq, gd), F32) + _nbytes((n, gd), F32) + 24 * _nbytes((SSD_CHUNK, gd), F32))
    return pl.pallas_call(
        _ssd_scan_kernel,
        grid=(batch, g),
        in_specs=[pl.BlockSpec((1, seq, gd), lambda b, j: (b, 0, j)),
                  pl.BlockSpec((1, seq, n), lambda b, j: (b, 0, bcol + j)),
                  pl.BlockSpec((1, seq, n), lambda b, j: (b, 0, ccol + j)),
                  pl.BlockSpec((1, seq, gd), lambda b, j: (b, 0, j)),
                  pl.BlockSpec((hg, seq), lambda b, j: (j, b)),
                  pl.BlockSpec((hg, seq), lambda b, j: (g + j, b)),
                  per_dir(0), per_dir(1), per_dir(0), per_dir(1),
                  pl.BlockSpec((1, gd), lambda b, j: (0, j))],
        out_specs=[pl.BlockSpec((1, seq, gd), lambda b, j: (b, 0, j)),
                   pl.BlockSpec((1, seq, V7X_LANES), lambda b, j: (b, 0, j))],
        out_shape=[jax.ShapeDtypeStruct((batch, seq, D_INNER), BF16),
                   jax.ShapeDtypeStruct((batch, seq, g * V7X_LANES), F32)],
        scratch_shapes=[pltpu.VMEM((seq, gd), F32), pltpu.VMEM((n, gd), F32)]
        + [pltpu.VMEM((rows, SSD_CHUNK), F32)] * 2 + [pltpu.VMEM((SSD_CHUNK, rows), F32)] * 3
        + [pltpu.VMEM((rows, gd), F32)],
        compiler_params=_params(("parallel", "parallel"), vmem),
        name="ssd_scan",
    )(xbc, xbc, xbc, zxbc, dt_t, dt_t, col(dt_bias), col(dt_bias), col(a_log), col(a_log),
      dskip_lanes)


def _ssd_mixer(x, norm_g, w_in, conv_w, conv_b, dt_bias, a_log, d_skip, ssd_norm, w_out, *,
               batch, seq):
    w_zxbc = w_in[:, :SSD_ZXBC_DIM].astype(BF16)
    w_dt_t = w_in[:, SSD_ZXBC_DIM:].T.astype(BF16)
    zxbc = _rms_matmul(x, norm_g, w_zxbc, tm=1024, tn=1024, out_dtype=BF16)
    dt_t = _rms_matmul_t(x, norm_g, w_dt_t, tm=1024)
    zxbc3 = zxbc.reshape(batch, seq, SSD_ZXBC_DIM)
    xbc = _conv_silu(zxbc3, conv_w, conv_b, batch=batch, seq=seq, tc=512, rows=64)
    yg, ssq = _ssd_scan(xbc, zxbc3, dt_t, dt_bias, a_log, d_skip, batch=batch, seq=seq)
    return _norm_matmul_res(yg.reshape(batch * seq, D_INNER), ssq.reshape(batch * seq, -1),
                            ssd_norm, w_out.astype(BF16), x, tm=1024, tn=512)


def _pool_kernel(u_ref, wg_ref, sc_ref, o_ref, pad_ref, *, rows):
    l, c = u_ref.shape[1], u_ref.shape[2]
    pad_ref[0:CONV_PAD_ROWS, :] = jnp.zeros((CONV_PAD_ROWS, c), F32)
    pad_ref[l + CONV_PAD_ROWS:l + 2 * CONV_PAD_ROWS, :] = jnp.zeros((CONV_PAD_ROWS, c), F32)
    pad_ref[CONV_PAD_ROWS:l + CONV_PAD_ROWS, :] = u_ref[0].astype(F32)

    for gi, win in enumerate(POOL_WINDOWS):
        half = win // 2

        @pl.when(pl.program_id(1) == gi)
        def _(half=half):
            def body(r, carry):
                r0 = pl.multiple_of(r * rows, rows)
                t = r0 + lax.broadcasted_iota(jnp.int32, (rows, 1), 0)
                cnt = jnp.minimum(t + half, l) - jnp.maximum(t - half, 0)
                inv = 1.0 / cnt.astype(F32)
                win = pad_ref[pl.ds(r0, rows + 2 * CONV_PAD_ROWS), :]
                s = win[CONV_PAD_ROWS - half:CONV_PAD_ROWS - half + rows, :]
                for k in range(1 - half, half):
                    s = s + win[CONV_PAD_ROWS + k:CONV_PAD_ROWS + k + rows, :]
                mix = (s * inv - win[CONV_PAD_ROWS:CONV_PAD_ROWS + rows, :]).astype(BF16)
                v = jnp.dot(mix, wg_ref[0], preferred_element_type=F32) * sc_ref[...]
                o_ref[0, pl.ds(r0, rows), :] = v.astype(o_ref.dtype)
                return carry

            lax.fori_loop(0, l // rows, body, 0)


def _pool(u, w_group, scale, *, batch, seq, rows):
    ng, gd = len(POOL_WINDOWS), POOL_GROUP_DIM
    vmem = (2 * (_nbytes((seq, gd), F32) + _nbytes((gd, gd), BF16) + _nbytes((seq, gd), BF16))
            + _nbytes((seq + 2 * CONV_PAD_ROWS, gd), F32) + 8 * _nbytes((rows, gd), F32))
    return pl.pallas_call(
        functools.partial(_pool_kernel, rows=rows),
        grid=(batch, ng),
        in_specs=[pl.BlockSpec((1, seq, gd), lambda b, j: (b, 0, j)),
                  pl.BlockSpec((1, gd, gd), lambda b, j: (j, 0, 0)),
                  pl.BlockSpec((1, gd), lambda b, j: (0, j))],
        out_specs=pl.BlockSpec((1, seq, gd), lambda b, j: (b, 0, j)),
        out_shape=jax.ShapeDtypeStruct((batch, seq, ng * gd), BF16),
        scratch_shapes=[pltpu.VMEM((seq + 2 * CONV_PAD_ROWS, gd), F32)],
        compiler_params=_params(("parallel", "parallel"), vmem),
        name="pool",
    )(u, w_group, scale.reshape(1, ng * gd))


def _pool_mixer(x, norm_g, w_in, w_group, scale, w_out, *, batch, seq):
    u = _rms_matmul(x, norm_g, w_in.astype(BF16), tm=1024, tn=1024, out_dtype=F32)
    v = _pool(u.reshape(batch, seq, D_MODEL), w_group.astype(BF16), scale,
              batch=batch, seq=seq, rows=128)
    return _matmul_res(v.reshape(batch * seq, D_MODEL), w_out.astype(BF16), x, tm=1024, tn=512)


def _rmsnorm_kernel(x_ref, g_ref, o_ref):
    o_ref[...] = _rmsnorm_rows(x_ref[...], g_ref[...])


def _rmsnorm(x, g, *, tm):
    t, d = x.shape
    return pl.pallas_call(
        _rmsnorm_kernel,
        grid=(t // tm,),
        in_specs=[pl.BlockSpec((tm, d), lambda i: (i, 0)), pl.BlockSpec((1, d), lambda i: (0, 0))],
        out_specs=pl.BlockSpec((tm, d), lambda i: (i, 0)),
        out_shape=jax.ShapeDtypeStruct((t, d), F32),
        compiler_params=_params(("parallel",), 6 * _nbytes((tm, d), F32)),
        name="final_rmsnorm",
    )(x, g.reshape(1, d))


def kernel(x, ffn_norm, ffn_w_gate, ffn_w_up, ffn_w_down, mix_norm, ssd_w_in, ssd_conv_w,
           ssd_conv_b, ssd_dt_bias, ssd_a_log, ssd_d, ssd_norm, ssd_w_out, pool_w_in, pool_w_group,
           pool_scale, pool_w_out, final_norm):
    batch, seq, d = x.shape
    depth = ffn_norm.shape[0]
    h = x.reshape(batch * seq, d)

    def ffn(h, i, s):
        return _ffn(h, ffn_norm[i, s], ffn_w_gate[i, s].astype(BF16), ffn_w_up[i, s].astype(BF16),
                    ffn_w_down[i, s].astype(BF16), tm=1024, tf=512)

    for i in range(depth):
        h = ffn(h, i, 0)
        j = i // 2
        if i % 2 == 0:
            h = _ssd_mixer(h, mix_norm[i], ssd_w_in[j], ssd_conv_w[j], ssd_conv_b[j], ssd_dt_bias[j],
                           ssd_a_log[j], ssd_d[j], ssd_norm[j], ssd_w_out[j], batch=batch, seq=seq)
        else:
            h = _pool_mixer(h, mix_norm[i], pool_w_in[j], pool_w_group[j], pool_scale[j],
                            pool_w_out[j], batch=batch, seq=seq)
        h = ffn(h, i, 1)
    return _rmsnorm(h, final_norm, tm=512).reshape(batch, seq, d)
```

```python
import functools

import jax
import jax.numpy as jnp
from jax import lax
from jax.experimental import pallas as pl
from jax.experimental.pallas import tpu as pltpu

F32 = jnp.float32
BF16 = jnp.bfloat16

EPS = 1e-6
D_MODEL = 2048
D_FF = 5632
D_INNER = 2 * D_MODEL
SSD_HEAD_DIM = 64
SSD_HEADS = D_INNER // SSD_HEAD_DIM
SSD_GROUPS = 8
SSD_HEADS_PER_GROUP = SSD_HEADS // SSD_GROUPS
SSD_GROUP_DIM = SSD_HEADS_PER_GROUP * SSD_HEAD_DIM
SSD_STATE = 128
SSD_CONV_W = 5
SSD_CHUNK = 128
SSD_BC_DIM = SSD_GROUPS * SSD_STATE
SSD_ZXBC_DIM = 2 * D_INNER + 2 * SSD_BC_DIM
POOL_WINDOWS = (2, 4, 8, 16)
POOL_GROUP_DIM = D_MODEL // len(POOL_WINDOWS)

V7X_VMEM_BYTES = 64 * 1024 * 1024
V7X_LANES = 128
V7X_SUBLANES = 8
BF16_SUBLANES = 2 * V7X_SUBLANES
V7X_MXU_WIDTH = 256
CONV_PAD_ROWS = V7X_SUBLANES
NORM_ROW_CHUNK = 128
CAST_LANES = 8 * V7X_LANES


def _params(semantics, vmem_bytes):
    return pltpu.CompilerParams(dimension_semantics=semantics,
                                vmem_limit_bytes=min(int(vmem_bytes), V7X_VMEM_BYTES))


def _nbytes(shape, dtype):
    n = 1
    for s in shape:
        n *= s
    return n * jnp.dtype(dtype).itemsize


def _rmsnorm_rows(x, g):
    ms = jnp.mean(x * x, axis=-1, keepdims=True)
    return x * lax.rsqrt(ms + EPS) * g


def _silu(v):
    return v * jax.nn.sigmoid(v)


def _cast_slabs(arrays, ni, nj):
    slabs, specs, vmem = [], [], 0
    for w in arrays:
        rows = w.size // CAST_LANES
        assert rows * CAST_LANES == w.size
        br = next(b for b in range(BF16_SUBLANES, rows + 1, BF16_SUBLANES)
                  if rows % b == 0 and rows // b <= ni * nj)
        nb = rows // br
        slabs.append(w.reshape(nb, br, CAST_LANES))
        specs.append(pl.BlockSpec((1, br, CAST_LANES),
                                  lambda i, j, nb=nb: (jnp.minimum(i * nj + j, nb - 1), 0, 0)))
        vmem += 2 * (_nbytes((br, CAST_LANES), F32) + _nbytes((br, CAST_LANES), BF16))
    return slabs, specs, vmem


def _run_casts(cast_in, cast_out):
    for src, dst in zip(cast_in, cast_out):
        dst[...] = src[...].astype(BF16)


def _rms_matmul_kernel(x_ref, g_ref, w_ref, *rest, n_cast):
    cast_in, o_ref = rest[:n_cast], rest[n_cast]
    cast_out, xn_ref = rest[n_cast + 1:2 * n_cast + 1], rest[2 * n_cast + 1]

    @pl.when(pl.program_id(1) == 0)
    def _():
        xn_ref[...] = _rmsnorm_rows(x_ref[...], g_ref[...]).astype(BF16)

    o_ref[...] = jnp.dot(xn_ref[...], w_ref[...], preferred_element_type=F32).astype(o_ref.dtype)
    _run_casts(cast_in, cast_out)


def _rms_matmul(x, g, w, cast_next=(), *, n, tm, tn, out_dtype):
    t, d = x.shape
    ni, nj = t // tm, n // tn
    slabs, slab_specs, slab_vmem = _cast_slabs(cast_next, ni, nj)
    vmem = (2 * (_nbytes((tm, d), F32) + _nbytes((d, tn), BF16) + _nbytes((tm, tn), out_dtype))
            + _nbytes((tm, d), BF16) + _nbytes((tm, d), F32) + _nbytes((tm, tn), F32) + slab_vmem)
    outs = pl.pallas_call(
        functools.partial(_rms_matmul_kernel, n_cast=len(slabs)),
        grid=(ni, nj),
        in_specs=[pl.BlockSpec((tm, d), lambda i, j: (i, 0)),
                  pl.BlockSpec((1, d), lambda i, j: (0, 0)),
                  pl.BlockSpec((d, tn), lambda i, j: (0, j))] + slab_specs,
        out_specs=[pl.BlockSpec((tm, tn), lambda i, j: (i, j))] + slab_specs,
        out_shape=[jax.ShapeDtypeStruct((t, n), out_dtype)]
        + [jax.ShapeDtypeStruct(s.shape, BF16) for s in slabs],
        scratch_shapes=[pltpu.VMEM((tm, d), BF16)],
        compiler_params=_params(("parallel", "arbitrary"), vmem),
        name="rms_matmul",
    )(x, g.reshape(1, d), w, *slabs)
    return outs[0], [c.reshape(a.shape) for c, a in zip(outs[1:], cast_next)]


def _rms_matmul_t_kernel(x_ref, g_ref, wt_ref, o_ref):
    xn = _rmsnorm_rows(x_ref[...], g_ref[...]).astype(BF16)
    o_ref[...] = lax.dot_general(wt_ref[...], xn, (((1,), (1,)), ((), ())),
                                 preferred_element_type=F32)


def _rms_matmul_t(x, g, wt, *, tm):
    t, d = x.shape
    n = wt.shape[0]
    vmem = (2 * (_nbytes((tm, d), F32) + _nbytes((n, d), BF16) + _nbytes((n, tm), F32))
            + 2 * _nbytes((tm, d), F32))
    return pl.pallas_call(
        _rms_matmul_t_kernel,
        grid=(t // tm,),
        in_specs=[pl.BlockSpec((tm, d), lambda i: (i, 0)),
                  pl.BlockSpec((1, d), lambda i: (0, 0)),
                  pl.BlockSpec((n, d), lambda i: (0, 0))],
        out_specs=pl.BlockSpec((n, tm), lambda i: (0, i)),
        out_shape=jax.ShapeDtypeStruct((n, t), F32),
        compiler_params=_params(("parallel",), vmem),
        name="rms_matmul_t",
    )(x, g.reshape(1, d), wt)


def _ffn_kernel(x_ref, g_ref, wg_ref, wu_ref, wd_ref, *rest, n_cast):
    cast_in, o_ref = rest[:n_cast], rest[n_cast]
    cast_out, xn_ref = rest[n_cast + 1:2 * n_cast + 1], rest[2 * n_cast + 1]

    @pl.when(pl.program_id(1) == 0)
    def _():
        x = x_ref[...]
        xn_ref[...] = _rmsnorm_rows(x, g_ref[...]).astype(BF16)
        o_ref[...] = x

    xn = xn_ref[...]
    gate = jnp.dot(xn, wg_ref[...], preferred_element_type=F32)
    up = jnp.dot(xn, wu_ref[...], preferred_element_type=F32)
    hidden = (_silu(gate) * up * 0.5).astype(BF16)
    o_ref[...] += jnp.dot(hidden, wd_ref[...], preferred_element_type=F32)
    _run_casts(cast_in, cast_out)


def _ffn(x, g, wg, wu, wd, cast_next, *, tm, tf):
    t, d = x.shape
    f = wg.shape[1]
    ni, nj = t // tm, f // tf
    slabs, slab_specs, slab_vmem = _cast_slabs(cast_next, ni, nj)
    vmem = (2 * (2 * _nbytes((tm, d), F32) + 2 * _nbytes((d, tf), BF16) + _nbytes((tf, d), BF16))
            + _nbytes((tm, d), BF16) + 4 * _nbytes((tm, tf), F32) + slab_vmem)
    outs = pl.pallas_call(
        functools.partial(_ffn_kernel, n_cast=len(slabs)),
        grid=(ni, nj),
        in_specs=[pl.BlockSpec((tm, d), lambda i, j: (i, 0)),
                  pl.BlockSpec((1, d), lambda i, j: (0, 0)),
                  pl.BlockSpec((d, tf), lambda i, j: (0, j)),
                  pl.BlockSpec((d, tf), lambda i, j: (0, j)),
                  pl.BlockSpec((tf, d), lambda i, j: (j, 0))] + slab_specs,
        out_specs=[pl.BlockSpec((tm, d), lambda i, j: (i, 0))] + slab_specs,
        out_shape=[jax.ShapeDtypeStruct((t, d), F32)]
        + [jax.ShapeDtypeStruct(s.shape, BF16) for s in slabs],
        scratch_shapes=[pltpu.VMEM((tm, d), BF16)],
        compiler_params=_params(("parallel", "arbitrary"), vmem),
        name="ffn",
    )(x, g.reshape(1, d), wg, wu, wd, *slabs)
    return outs[0], [c.reshape(w.shape) for c, w in zip(outs[1:], cast_next)]


def _matmul_res_kernel(a_ref, w_ref, r_ref, o_ref):
    o_ref[...] = r_ref[...] + jnp.dot(a_ref[...], w_ref[...], preferred_element_type=F32)


def _matmul_res(a, w, res, *, tm, tn):
    t, k = a.shape
    n = w.shape[1]
    vmem = 2 * (_nbytes((tm, k), a.dtype) + _nbytes((k, tn), BF16) + 2 * _nbytes((tm, tn), F32)) \
        + _nbytes((tm, tn), F32)
    return pl.pallas_call(
        _matmul_res_kernel,
        grid=(t // tm, n // tn),
        in_specs=[pl.BlockSpec((tm, k), lambda i, j: (i, 0)),
                  pl.BlockSpec((k, tn), lambda i, j: (0, j)),
                  pl.BlockSpec((tm, tn), lambda i, j: (i, j))],
        out_specs=pl.BlockSpec((tm, tn), lambda i, j: (i, j)),
        out_shape=jax.ShapeDtypeStruct((t, n), F32),
        compiler_params=_params(("parallel", "arbitrary"), vmem),
        name="matmul_res",
    )(a, w, res)


def _norm_matmul_res_kernel(yg_ref, ssq_ref, g_ref, w_ref, r_ref, o_ref, an_ref):
    @pl.when(pl.program_id(1) == 0)
    def _():
        rc = NORM_ROW_CHUNK

        def body(r, carry):
            rows = pl.ds(pl.multiple_of(r * rc, rc), rc)
            ms = jnp.sum(ssq_ref[rows, :], axis=-1, keepdims=True) * (1.0 / yg_ref.shape[-1])
            an_ref[rows, :] = (yg_ref[rows, :].astype(F32) * lax.rsqrt(ms + EPS)
                               * g_ref[...]).astype(BF16)
            return carry

        lax.fori_loop(0, yg_ref.shape[0] // rc, body, 0)

    o_ref[...] = r_ref[...] + jnp.dot(an_ref[...], w_ref[...], preferred_element_type=F32)


def _norm_matmul_res(yg, ssq, g, w, res, *, tm, tn):
    t, k = yg.shape
    n = w.shape[1]
    s = ssq.shape[1]
    vmem = (2 * (_nbytes((tm, k), yg.dtype) + _nbytes((tm, s), F32) + _nbytes((k, tn), BF16)
                 + 2 * _nbytes((tm, tn), F32))
            + _nbytes((tm, k), BF16) + 4 * _nbytes((tm, tn), F32))
    return pl.pallas_call(
        _norm_matmul_res_kernel,
        grid=(t // tm, n // tn),
        in_specs=[pl.BlockSpec((tm, k), lambda i, j: (i, 0)),
                  pl.BlockSpec((tm, s), lambda i, j: (i, 0)),
                  pl.BlockSpec((1, k), lambda i, j: (0, 0)),
                  pl.BlockSpec((k, tn), lambda i, j: (0, j)),
                  pl.BlockSpec((tm, tn), lambda i, j: (i, j))],
        out_specs=pl.BlockSpec((tm, tn), lambda i, j: (i, j)),
        out_shape=jax.ShapeDtypeStruct((t, n), F32),
        scratch_shapes=[pltpu.VMEM((tm, k), BF16)],
        compiler_params=_params(("parallel", "arbitrary"), vmem),
        name="norm_matmul_res",
    )(yg, ssq, g.reshape(1, k), w, res)


def _conv_silu_kernel(u_ref, w_ref, b_ref, o_ref, pad_ref, *, rows):
    l, c = u_ref.shape[1], u_ref.shape[2]
    pad_ref[0:CONV_PAD_ROWS, :] = jnp.zeros((CONV_PAD_ROWS, c), F32)
    pad_ref[l + CONV_PAD_ROWS:l + 2 * CONV_PAD_ROWS, :] = jnp.zeros((CONV_PAD_ROWS, c), F32)
    pad_ref[CONV_PAD_ROWS:l + CONV_PAD_ROWS, :] = u_ref[0].astype(F32)
    half = SSD_CONV_W // 2

    def body(r, carry):
        r0 = pl.multiple_of(r * rows, rows)
        win = pad_ref[pl.ds(r0, rows + 2 * CONV_PAD_ROWS), :]
        acc = jnp.broadcast_to(b_ref[...], (rows, c))
        for k in range(SSD_CONV_W):
            lo = CONV_PAD_ROWS + k - half
            acc = acc + win[lo:lo + rows, :] * w_ref[k:k + 1, :]
        o_ref[0, pl.ds(r0, rows), :] = _silu(acc)
        return carry

    lax.fori_loop(0, l // rows, body, 0)


def _conv_silu(zxbc, conv_w, conv_b, *, batch, seq, tc, rows):
    cdim = conv_w.shape[1]
    col0 = D_INNER // tc
    vmem = 4 * _nbytes((seq, tc), F32) + _nbytes((seq + 2 * CONV_PAD_ROWS, tc), F32) \
        + 8 * _nbytes((rows, tc), F32)
    return pl.pallas_call(
        functools.partial(_conv_silu_kernel, rows=rows),
        grid=(batch, cdim // tc),
        in_specs=[pl.BlockSpec((1, seq, tc), lambda b, j: (b, 0, col0 + j)),
                  pl.BlockSpec((SSD_CONV_W, tc), lambda b, j: (0, j)),
                  pl.BlockSpec((1, tc), lambda b, j: (0, j))],
        out_specs=pl.BlockSpec((1, seq, tc), lambda b, j: (b, 0, j)),
        out_shape=jax.ShapeDtypeStruct((batch, seq, cdim), F32),
        scratch_shapes=[pltpu.VMEM((seq + 2 * CONV_PAD_ROWS, tc), F32)],
        compiler_params=_params(("parallel", "parallel"), vmem),
        name="conv_silu",
    )(zxbc, conv_w, conv_b.reshape(1, cdim))


def _split_dot(v, m01):
    hi = v.astype(BF16)
    r1 = v - hi.astype(F32)
    mid = r1.astype(BF16)
    lo = (r1 - mid.astype(F32)).astype(BF16)
    out = jnp.dot(hi, m01, preferred_element_type=F32)
    out = out + jnp.dot(mid, m01, preferred_element_type=F32)
    return out + jnp.dot(lo, m01, preferred_element_type=F32)


def _softplus(v):
    return jnp.maximum(v, 0.0) + jnp.log1p(jnp.exp(-jnp.abs(v)))


def _log2(n):
    assert n & (n - 1) == 0
    return n.bit_length() - 1


def _ssd_prepare(reverse, dt_ref, bias_ref, alog_ref, r_cs, r_dt, t_cs, t_w, t_e, dec_ref):
    q, hg, p = SSD_CHUNK, SSD_HEADS_PER_GROUP, SSD_HEAD_DIM
    nc = dt_ref.shape[1] // q
    rows = nc * hg
    dt_all = _softplus(dt_ref[...] + bias_ref[...])
    dta_all = dt_all * (-jnp.exp(alog_ref[...]))
    stack = lambda v: jnp.concatenate([v[:, c * q:(c + 1) * q] for c in range(nc)], axis=0)
    dt_r = stack(dt_all)
    src = lax.broadcasted_iota(jnp.int32, (q, q), 0)
    dst = lax.broadcasted_iota(jnp.int32, (q, q), 1)
    cum01 = ((src >= dst) if reverse else (src <= dst)).astype(BF16)
    last_pos = 0 if reverse else q - 1
    cs = _split_dot(stack(dta_all), cum01)
    cs_last = cs[:, last_pos:last_pos + 1]
    cs_t = cs.T
    r_cs[...] = cs
    r_dt[...] = dt_r
    t_cs[...] = cs_t
    t_w[...] = (dt_r * jnp.exp(cs_last - cs)).T
    t_e[...] = jnp.exp(cs).T
    band_row = lax.shift_right_logical(lax.broadcasted_iota(jnp.int32, (rows, rows), 0), _log2(hg))
    band_col = lax.shift_right_logical(lax.broadcasted_iota(jnp.int32, (rows, rows), 1), _log2(hg))
    dec_sel = jnp.where(band_row == band_col, jnp.exp(cs_t[last_pos:last_pos + 1, :]), 0.0)
    head_of_row = lax.broadcasted_iota(jnp.int32, (rows, SSD_GROUP_DIM), 0) & (hg - 1)
    head_of_lane = lax.shift_right_logical(
        lax.broadcasted_iota(jnp.int32, (rows, SSD_GROUP_DIM), 1), _log2(p))
    dec_ref[...] = _split_dot(dec_sel, (head_of_row == head_of_lane).astype(BF16))


def _ssd_direction(reverse, x_ref, b_ref, c_ref, h_ref, r_cs, r_dt, t_cs, t_w, t_e, dec_ref, emit):
    q, hg, p, gd = SSD_CHUNK, SSD_HEADS_PER_GROUP, SSD_HEAD_DIM, SSD_GROUP_DIM
    nc = x_ref.shape[1] // q
    rows = nc * hg
    per_tile = V7X_MXU_WIDTH // p
    tgt = lax.broadcasted_iota(jnp.int32, (q, q), 0)
    src = lax.broadcasted_iota(jnp.int32, (q, q), 1)
    valid = (tgt <= src) if reverse else (tgt >= src)
    assert rows == V7X_LANES and 2 * p == V7X_LANES
    lane = lax.broadcasted_iota(jnp.int32, (q, V7X_LANES), 1)
    second_of_pair = lax.shift_right_logical(lane, _log2(p))
    tile_head = lax.shift_right_logical(
        lax.broadcasted_iota(jnp.int32, (q, V7X_MXU_WIDTH), 1), _log2(p))
    h_ref[...] = jnp.zeros(h_ref.shape, F32)

    def body(step, carry):
        chunk = (nc - 1 - step) if reverse else step
        r0 = pl.multiple_of(chunk * q, q)
        band = pl.ds(pl.multiple_of(chunk * hg, hg), hg)
        lane0 = chunk * hg

        def spread(t_ref):
            cols = t_ref[...]
            return jnp.concatenate(
                [jnp.take_along_axis(cols, lane0 + k + second_of_pair, axis=1)
                 for k in range(0, hg, 2)], axis=1)

        cs_all = t_cs[...]
        cs_rows = r_cs[band, :]
        dt_rows = r_dt[band, :]
        dec = dec_ref[band, :][0:1, :]

        x_c = x_ref[0, pl.ds(r0, q), :]
        x_bf = x_c.astype(BF16)
        b_c = b_ref[0, pl.ds(r0, q), :]
        c_bf = c_ref[0, pl.ds(r0, q), :].astype(BF16)
        cb = lax.dot_general(c_bf, b_c.astype(BF16), (((1,), (1,)), ((), ())),
                             preferred_element_type=F32)
        h_in = h_ref[...]
        y = jnp.dot(c_bf, h_in.astype(BF16), preferred_element_type=F32) * spread(t_e)
        wx = (x_c * spread(t_w)).astype(BF16)
        h_ref[...] = h_in * dec + jnp.dot(b_c.T.astype(BF16), wx, preferred_element_type=F32)

        y_tiles = []
        for t0 in range(0, hg, per_tile):
            ms, xs = [], []
            x_tile = x_bf[:, t0 * p:(t0 + per_tile) * p]
            for k in range(per_tile):
                h = t0 + k
                cs_tgt = jnp.take_along_axis(cs_all, jnp.broadcast_to(lane0 + h, lane.shape), axis=1)
                seg = cs_tgt - cs_rows[h:h + 1, :]
                lmat = jnp.exp(jnp.where(valid, seg, -jnp.inf))
                ms.append((cb * lmat * dt_rows[h:h + 1, :]).astype(BF16))
                xs.append(jnp.where(tile_head == k, x_tile, jnp.zeros_like(x_tile)))
            y_tiles.append(jnp.dot(jnp.concatenate(ms, axis=1), jnp.concatenate(xs, axis=0),
                                   preferred_element_type=F32))
        emit(r0, x_c, y + jnp.concatenate(y_tiles, axis=1))
        return carry

    lax.fori_loop(0, nc, body, 0, unroll=2)


def _ssd_scan_kernel(x_ref, b_ref, c_ref, z_ref, dtf_ref, dtb_ref, biasf_ref, biasb_ref,
                     alogf_ref, alogb_ref, dskip_ref, yg_ref, ssq_ref,
                     y_acc, h_ref, r_cs, r_dt, t_cs, t_w, t_e, dec_ref):
    q = SSD_CHUNK
    decay = (r_cs, r_dt, t_cs, t_w, t_e, dec_ref)

    def emit_fwd(r0, x_c, y):
        y_acc[pl.ds(r0, q), :] = y + x_c * dskip_ref[...]

    def emit_bwd(r0, x_c, y):
        yg = (y_acc[pl.ds(r0, q), :] + y) * _silu(z_ref[0, pl.ds(r0, q), :].astype(F32))
        yg_ref[0, pl.ds(r0, q), :] = yg.astype(yg_ref.dtype)
        sq = yg * yg
        folded = sq[:, 0:V7X_LANES]
        for k in range(1, SSD_GROUP_DIM // V7X_LANES):
            folded = folded + sq[:, k * V7X_LANES:(k + 1) * V7X_LANES]
        ssq_ref[0, pl.ds(r0, q), :] = folded

    _ssd_prepare(False, dtf_ref, biasf_ref, alogf_ref, *decay)
    _ssd_direction(False, x_ref, b_ref, c_ref, h_ref, *decay, emit_fwd)
    _ssd_prepare(True, dtb_ref, biasb_ref, alogb_ref, *decay)
    _ssd_direction(True, x_ref, b_ref, c_ref, h_ref, *decay, emit_bwd)


def _ssd_scan(xbc, zxbc, dt_t, dt_bias, a_log, d_skip, *, batch, seq):
    g, hg, gd, n = SSD_GROUPS, SSD_HEADS_PER_GROUP, SSD_GROUP_DIM, SSD_STATE
    bcol = D_INNER // n
    ccol = (D_INNER + SSD_BC_DIM) // n
    col = lambda v: v.reshape(2 * SSD_HEADS, 1)
    dskip_lanes = jnp.repeat(d_skip, SSD_HEAD_DIM).reshape(1, D_INNER)
    per_dir = lambda d: pl.BlockSpec((hg, 1), lambda b, j: (d * g + j, 0))
    rows = (seq // SSD_CHUNK) * hg
    vmem = (2 * (3 * _nbytes((seq, gd), F32) + 3 * _nbytes((seq, n), F32) + 2 * _nbytes((hg, seq), F32))
            + _nbytes((seq, gd), F32) + _nbytes((n, gd), F32) + 24 * _nbytes((SSD_CHUNK, gd), F32))
    return pl.pallas_call(
        _ssd_scan_kernel,
        grid=(batch, g),
        in_specs=[pl.BlockSpec((1, seq, gd), lambda b, j: (b, 0, j)),
                  pl.BlockSpec((1, seq, n), lambda b, j: (b, 0, bcol + j)),
                  pl.BlockSpec((1, seq, n), lambda b, j: (b, 0, ccol + j)),
                  pl.BlockSpec((1, seq, gd), lambda b, j: (b, 0, j)),
                  pl.BlockSpec((hg, seq), lambda b, j: (j, b)),
                  pl.BlockSpec((hg, seq), lambda b, j: (g + j, b)),
                  per_dir(0), per_dir(1), per_dir(0), per_dir(1),
                  pl.BlockSpec((1, gd), lambda b, j: (0, j))],
        out_specs=[pl.BlockSpec((1, seq, gd), lambda b, j: (b, 0, j)),
                   pl.BlockSpec((1, seq, V7X_LANES), lambda b, j: (b, 0, j))],
        out_shape=[jax.ShapeDtypeStruct((batch, seq, D_INNER), BF16),
                   jax.ShapeDtypeStruct((batch, seq, g * V7X_LANES), F32)],
        scratch_shapes=[pltpu.VMEM((seq, gd), F32), pltpu.VMEM((n, gd), F32)]
        + [pltpu.VMEM((rows, SSD_CHUNK), F32)] * 2 + [pltpu.VMEM((SSD_CHUNK, rows), F32)] * 3
        + [pltpu.VMEM((rows, gd), F32)],
        compiler_params=_params(("parallel", "parallel"), vmem),
        name="ssd_scan",
    )(xbc, xbc, xbc, zxbc, dt_t, dt_t, col(dt_bias), col(dt_bias), col(a_log), col(a_log),
      dskip_lanes)


def _ssd_mixer(x, norm_g, w_in_bf, conv_w, conv_b, dt_bias, a_log, d_skip, ssd_norm, w_out, *,
               batch, seq):
    w_dt_t = w_in_bf[:, SSD_ZXBC_DIM:].T
    zxbc, (w_out_bf,) = _rms_matmul(x, norm_g, w_in_bf, [w_out], n=SSD_ZXBC_DIM, tm=1024, tn=1024,
                                    out_dtype=BF16)
    dt_t = _rms_matmul_t(x, norm_g, w_dt_t, tm=1024)
    zxbc3 = zxbc.reshape(batch, seq, SSD_ZXBC_DIM)
    xbc = _conv_silu(zxbc3, conv_w, conv_b, batch=batch, seq=seq, tc=512, rows=64)
    yg, ssq = _ssd_scan(xbc, zxbc3, dt_t, dt_bias, a_log, d_skip, batch=batch, seq=seq)
    return _norm_matmul_res(yg.reshape(batch * seq, D_INNER), ssq.reshape(batch * seq, -1),
                            ssd_norm, w_out_bf, x, tm=1024, tn=512)


def _pool_kernel(u_ref, wg_ref, sc_ref, o_ref, pad_ref, *, rows):
    l, c = u_ref.shape[1], u_ref.shape[2]
    pad_ref[0:CONV_PAD_ROWS, :] = jnp.zeros((CONV_PAD_ROWS, c), F32)
    pad_ref[l + CONV_PAD_ROWS:l + 2 * CONV_PAD_ROWS, :] = jnp.zeros((CONV_PAD_ROWS, c), F32)
    pad_ref[CONV_PAD_ROWS:l + CONV_PAD_ROWS, :] = u_ref[0].astype(F32)

    for gi, win in enumerate(POOL_WINDOWS):
        half = win // 2

        @pl.when(pl.program_id(1) == gi)
        def _(half=half):
            def body(r, carry):
                r0 = pl.multiple_of(r * rows, rows)
                t = r0 + lax.broadcasted_iota(jnp.int32, (rows, 1), 0)
                cnt = jnp.minimum(t + half, l) - jnp.maximum(t - half, 0)
                inv = 1.0 / cnt.astype(F32)
                win = pad_ref[pl.ds(r0, rows + 2 * CONV_PAD_ROWS), :]
                s = win[CONV_PAD_ROWS - half:CONV_PAD_ROWS - half + rows, :]
                for k in range(1 - half, half):
                    s = s + win[CONV_PAD_ROWS + k:CONV_PAD_ROWS + k + rows, :]
                mix = (s * inv - win[CONV_PAD_ROWS:CONV_PAD_ROWS + rows, :]).astype(BF16)
                v = jnp.dot(mix, wg_ref[0], preferred_element_type=F32) * sc_ref[...]
                o_ref[0, pl.ds(r0, rows), :] = v.astype(o_ref.dtype)
                return carry

            lax.fori_loop(0, l // rows, body, 0)


def _pool(u, w_group, scale, *, batch, seq, rows):
    ng, gd = len(POOL_WINDOWS), POOL_GROUP_DIM
    vmem = (2 * (_nbytes((seq, gd), F32) + _nbytes((gd, gd), BF16) + _nbytes((seq, gd), BF16))
            + _nbytes((seq + 2 * CONV_PAD_ROWS, gd), F32) + 8 * _nbytes((rows, gd), F32))
    return pl.pallas_call(
        functools.partial(_pool_kernel, rows=rows),
        grid=(batch, ng),
        in_specs=[pl.BlockSpec((1, seq, gd), lambda b, j: (b, 0, j)),
                  pl.BlockSpec((1, gd, gd), lambda b, j: (j, 0, 0)),
                  pl.BlockSpec((1, gd), lambda b, j: (0, j))],
        out_specs=pl.BlockSpec((1, seq, gd), lambda b, j: (b, 0, j)),
        out_shape=jax.ShapeDtypeStruct((batch, seq, ng * gd), BF16),
        scratch_shapes=[pltpu.VMEM((seq + 2 * CONV_PAD_ROWS, gd), F32)],
        compiler_params=_params(("parallel", "parallel"), vmem),
        name="pool",
    )(u, w_group, scale.reshape(1, ng * gd))


def _pool_mixer(x, norm_g, w_in_bf, w_group_bf, scale, w_out_bf, *, batch, seq):
    u, _ = _rms_matmul(x, norm_g, w_in_bf, n=D_MODEL, tm=1024, tn=1024, out_dtype=F32)
    v = _pool(u.reshape(batch, seq, D_MODEL), w_group_bf, scale, batch=batch, seq=seq, rows=128)
    return _matmul_res(v.reshape(batch * seq, D_MODEL), w_out_bf, x, tm=1024, tn=512)


def _rmsnorm_kernel(x_ref, g_ref, o_ref):
    o_ref[...] = _rmsnorm_rows(x_ref[...], g_ref[...])


def _rmsnorm(x, g, *, tm):
    t, d = x.shape
    return pl.pallas_call(
        _rmsnorm_kernel,
        grid=(t // tm,),
        in_specs=[pl.BlockSpec((tm, d), lambda i: (i, 0)), pl.BlockSpec((1, d), lambda i: (0, 0))],
        out_specs=pl.BlockSpec((tm, d), lambda i: (i, 0)),
        out_shape=jax.ShapeDtypeStruct((t, d), F32),
        compiler_params=_params(("parallel",), 6 * _nbytes((tm, d), F32)),
        name="final_rmsnorm",
    )(x, g.reshape(1, d))


def kernel(x, ffn_norm, ffn_w_gate, ffn_w_up, ffn_w_down, mix_norm, ssd_w_in, ssd_conv_w,
           ssd_conv_b, ssd_dt_bias, ssd_a_log, ssd_d, ssd_norm, ssd_w_out, pool_w_in, pool_w_group,
           pool_scale, pool_w_out, final_norm):
    batch, seq, d = x.shape
    depth = ffn_norm.shape[0]
    h = x.reshape(batch * seq, d)

    n_ffn = 2 * depth
    ffn_raw = lambda k: [w[k // 2, k % 2] for w in (ffn_w_gate, ffn_w_up, ffn_w_down)]
    ffn_w = [w.astype(BF16) for w in ffn_raw(0)]

    def ffn(h, k, w_bf, mixer_raw):
        nxt = (ffn_raw(k + 1) if k + 1 < n_ffn else []) + mixer_raw
        h, cast = _ffn(h, ffn_norm[k // 2, k % 2], *w_bf, nxt, tm=1024, tf=512)
        return h, cast[:len(nxt) - len(mixer_raw)], cast[len(nxt) - len(mixer_raw):]

    for i in range(depth):
        j = i // 2
        if i % 2 == 0:
            h, ffn_w, (w_in_bf,) = ffn(h, 2 * i, ffn_w, [ssd_w_in[j]])
            h = _ssd_mixer(h, mix_norm[i], w_in_bf, ssd_conv_w[j], ssd_conv_b[j], ssd_dt_bias[j],
                           ssd_a_log[j], ssd_d[j], ssd_norm[j], ssd_w_out[j], batch=batch, seq=seq)
        else:
            h, ffn_w, pool_w = ffn(h, 2 * i, ffn_w, [pool_w_in[j], pool_w_group[j], pool_w_out[j]])
            h = _pool_mixer(h, mix_norm[i], pool_w[0], pool_w[1], pool_scale[j], pool_w[2],
                            batch=batch, seq=seq)
        h, ffn_w, _ = ffn(h, 2 * i + 1, ffn_w, [])
    return _rmsnorm(h, final_norm, tm=512).reshape(batch, seq, d)
```

```python
import functools

import jax
import jax.numpy as jnp
from jax import lax
from jax.experimental import pallas as pl
from jax.experimental.pallas import tpu as pltpu

F32 = jnp.float32
BF16 = jnp.bfloat16

EPS = 1e-6
D_MODEL = 2048
D_FF = 5632
D_INNER = 2 * D_MODEL
SSD_HEAD_DIM = 64
SSD_HEADS = D_INNER // SSD_HEAD_DIM
SSD_GROUPS = 8
SSD_HEADS_PER_GROUP = SSD_HEADS // SSD_GROUPS
SSD_GROUP_DIM = SSD_HEADS_PER_GROUP * SSD_HEAD_DIM
SSD_STATE = 128
SSD_CONV_W = 5
SSD_CHUNK = 128
SSD_BC_DIM = SSD_GROUPS * SSD_STATE
SSD_ZXBC_DIM = 2 * D_INNER + 2 * SSD_BC_DIM
POOL_WINDOWS = (2, 4, 8, 16)
POOL_GROUP_DIM = D_MODEL // len(POOL_WINDOWS)

V7X_VMEM_BYTES = 64 * 1024 * 1024
V7X_LANES = 128
V7X_SUBLANES = 8
BF16_SUBLANES = 2 * V7X_SUBLANES
V7X_MXU_WIDTH = 256
CONV_PAD_ROWS = V7X_SUBLANES
NORM_ROW_CHUNK = 128

def _params(semantics, vmem_bytes):
    return pltpu.CompilerParams(dimension_semantics=semantics,
                                vmem_limit_bytes=min(int(vmem_bytes), V7X_VMEM_BYTES))


def _nbytes(shape, dtype):
    n = 1
    for s in shape:
        n *= s
    return n * jnp.dtype(dtype).itemsize


def _rmsnorm_rows(x, g):
    ms = jnp.mean(x * x, axis=-1, keepdims=True)
    return x * lax.rsqrt(ms + EPS) * g


def _silu(v):
    return v * jax.nn.sigmoid(v)


def _cast_slabs(arrays, ni, nj):
    in_specs, out_specs, out_shapes, vmem = [], [], [], 0
    for stack, lead in arrays:
        r, c = stack.shape[-2:]
        tiles = [(br, bc) for br in range(BF16_SUBLANES, r + 1, BF16_SUBLANES) if r % br == 0
                 for bc in range(V7X_LANES, c + 1, V7X_LANES) if c % bc == 0
                 and (r // br) * (c // bc) <= ni * nj]
        br, bc = min(tiles, key=lambda t: t[0] * t[1])
        ncb, nb = c // bc, (r // br) * (c // bc)

        def tile_index(i, j, ncb=ncb, nb=nb):
            blk = jnp.minimum(i * nj + j, nb - 1)
            return blk // ncb, blk % ncb

        in_specs.append(pl.BlockSpec((None,) * len(lead) + (br, bc),
                                     lambda i, j, lead=lead, f=tile_index: lead + f(i, j)))
        out_specs.append(pl.BlockSpec((br, bc), tile_index))
        out_shapes.append(jax.ShapeDtypeStruct((r, c), BF16))
        vmem += 2 * (_nbytes((br, bc), F32) + _nbytes((br, bc), BF16))
    return in_specs, out_specs, out_shapes, vmem


def _run_casts(cast_in, cast_out):
    for src, dst in zip(cast_in, cast_out):
        dst[...] = src[...].astype(BF16)


def _rms_matmul_kernel(x_ref, g_ref, w_ref, *rest, n_cast):
    cast_in, o_ref = rest[:n_cast], rest[n_cast]
    cast_out, xn_ref = rest[n_cast + 1:2 * n_cast + 1], rest[2 * n_cast + 1]

    @pl.when(pl.program_id(1) == 0)
    def _():
        xn_ref[...] = _rmsnorm_rows(x_ref[...], g_ref[...]).astype(BF16)

    o_ref[...] = jnp.dot(xn_ref[...], w_ref[...], preferred_element_type=F32).astype(o_ref.dtype)
    _run_casts(cast_in, cast_out)


def _rms_matmul(x, g, w, cast_next=(), *, n, tm, tn, out_dtype):
    t, d = x.shape
    ni, nj = t // tm, n // tn
    cast_in, cast_out, cast_shapes, cast_vmem = _cast_slabs(cast_next, ni, nj)
    vmem = (2 * (_nbytes((tm, d), F32) + _nbytes((d, tn), BF16) + _nbytes((tm, tn), out_dtype))
            + _nbytes((tm, d), BF16) + _nbytes((tm, d), F32) + _nbytes((tm, tn), F32) + cast_vmem)
    outs = pl.pallas_call(
        functools.partial(_rms_matmul_kernel, n_cast=len(cast_next)),
        grid=(ni, nj),
        in_specs=[pl.BlockSpec((tm, d), lambda i, j: (i, 0)),
                  pl.BlockSpec((1, d), lambda i, j: (0, 0)),
                  pl.BlockSpec((d, tn), lambda i, j: (0, j))] + cast_in,
        out_specs=[pl.BlockSpec((tm, tn), lambda i, j: (i, j))] + cast_out,
        out_shape=[jax.ShapeDtypeStruct((t, n), out_dtype)] + cast_shapes,
        scratch_shapes=[pltpu.VMEM((tm, d), BF16)],
        compiler_params=_params(("parallel", "arbitrary"), vmem),
        name="rms_matmul",
    )(x, g.reshape(1, d), w, *[stack for stack, _ in cast_next])
    return outs[0], outs[1:]


def _rms_matmul_t_kernel(x_ref, g_ref, wt_ref, o_ref):
    xn = _rmsnorm_rows(x_ref[...], g_ref[...]).astype(BF16)
    o_ref[...] = lax.dot_general(wt_ref[...], xn, (((1,), (1,)), ((), ())),
                                 preferred_element_type=F32)


def _rms_matmul_t(x, g, wt, *, tm):
    t, d = x.shape
    n = wt.shape[0]
    vmem = (2 * (_nbytes((tm, d), F32) + _nbytes((n, d), BF16) + _nbytes((n, tm), F32))
            + 2 * _nbytes((tm, d), F32))
    return pl.pallas_call(
        _rms_matmul_t_kernel,
        grid=(t // tm,),
        in_specs=[pl.BlockSpec((tm, d), lambda i: (i, 0)),
                  pl.BlockSpec((1, d), lambda i: (0, 0)),
                  pl.BlockSpec((n, d), lambda i: (0, 0))],
        out_specs=pl.BlockSpec((n, tm), lambda i: (0, i)),
        out_shape=jax.ShapeDtypeStruct((n, t), F32),
        compiler_params=_params(("parallel",), vmem),
        name="rms_matmul_t",
    )(x, g.reshape(1, d), wt)


def _ffn_kernel(x_ref, g_ref, wg_ref, wu_ref, wd_ref, *rest, n_cast):
    cast_in, o_ref = rest[:n_cast], rest[n_cast]
    cast_out, xn_ref = rest[n_cast + 1:2 * n_cast + 1], rest[2 * n_cast + 1]

    @pl.when(pl.program_id(1) == 0)
    def _():
        x = x_ref[...]
        xn_ref[...] = _rmsnorm_rows(x, g_ref[...]).astype(BF16)
        o_ref[...] = x

    xn = xn_ref[...]
    gate = jnp.dot(xn, wg_ref[...], preferred_element_type=F32)
    up = jnp.dot(xn, wu_ref[...], preferred_element_type=F32)
    hidden = (_silu(gate) * up * 0.5).astype(BF16)
    o_ref[...] += jnp.dot(hidden, wd_ref[...], preferred_element_type=F32)
    _run_casts(cast_in, cast_out)


def _ffn(x, g, wg, wu, wd, cast_next, *, tm, tf):
    t, d = x.shape
    f = wg.shape[1]
    ni, nj = t // tm, f // tf
    cast_in, cast_out, cast_shapes, cast_vmem = _cast_slabs(cast_next, ni, nj)
    vmem = (2 * (2 * _nbytes((tm, d), F32) + 2 * _nbytes((d, tf), BF16) + _nbytes((tf, d), BF16))
            + _nbytes((tm, d), BF16) + 4 * _nbytes((tm, tf), F32) + cast_vmem)
    outs = pl.pallas_call(
        functools.partial(_ffn_kernel, n_cast=len(cast_next)),
        grid=(ni, nj),
        in_specs=[pl.BlockSpec((tm, d), lambda i, j: (i, 0)),
                  pl.BlockSpec((1, d), lambda i, j: (0, 0)),
                  pl.BlockSpec((d, tf), lambda i, j: (0, j)),
                  pl.BlockSpec((d, tf), lambda i, j: (0, j)),
                  pl.BlockSpec((tf, d), lambda i, j: (j, 0))] + cast_in,
        out_specs=[pl.BlockSpec((tm, d), lambda i, j: (i, 0))] + cast_out,
        out_shape=[jax.ShapeDtypeStruct((t, d), F32)] + cast_shapes,
        scratch_shapes=[pltpu.VMEM((tm, d), BF16)],
        compiler_params=_params(("parallel", "arbitrary"), vmem),
        name="ffn",
    )(x, g.reshape(1, d), wg, wu, wd, *[stack for stack, _ in cast_next])
    return outs[0], outs[1:]


def _matmul_res_kernel(a_ref, w_ref, r_ref, o_ref):
    o_ref[...] = r_ref[...] + jnp.dot(a_ref[...], w_ref[...], preferred_element_type=F32)


def _matmul_res(a, w, res, *, tm, tn):
    t, k = a.shape
    n = w.shape[1]
    vmem = 2 * (_nbytes((tm, k), a.dtype) + _nbytes((k, tn), BF16) + 2 * _nbytes((tm, tn), F32)) \
        + _nbytes((tm, tn), F32)
    return pl.pallas_call(
        _matmul_res_kernel,
        grid=(t // tm, n // tn),
        in_specs=[pl.BlockSpec((tm, k), lambda i, j: (i, 0)),
                  pl.BlockSpec((k, tn), lambda i, j: (0, j)),
                  pl.BlockSpec((tm, tn), lambda i, j: (i, j))],
        out_specs=pl.BlockSpec((tm, tn), lambda i, j: (i, j)),
        out_shape=jax.ShapeDtypeStruct((t, n), F32),
        compiler_params=_params(("parallel", "arbitrary"), vmem),
        name="matmul_res",
    )(a, w, res)


def _norm_matmul_res_kernel(yg_ref, ssq_ref, g_ref, w_ref, r_ref, o_ref, an_ref):
    @pl.when(pl.program_id(1) == 0)
    def _():
        rc = NORM_ROW_CHUNK

        def body(r, carry):
            rows = pl.ds(pl.multiple_of(r * rc, rc), rc)
            ms = jnp.sum(ssq_ref[rows, :], axis=-1, keepdims=True) * (1.0 / yg_ref.shape[-1])
            an_ref[rows, :] = (yg_ref[rows, :].astype(F32) * lax.rsqrt(ms + EPS)
                               * g_ref[...]).astype(BF16)
            return carry

        lax.fori_loop(0, yg_ref.shape[0] // rc, body, 0)

    o_ref[...] = r_ref[...] + jnp.dot(an_ref[...], w_ref[...], preferred_element_type=F32)


def _norm_matmul_res(yg, ssq, g, w, res, *, tm, tn):
    t, k = yg.shape
    n = w.shape[1]
    s = ssq.shape[1]
    vmem = (2 * (_nbytes((tm, k), yg.dtype) + _nbytes((tm, s), F32) + _nbytes((k, tn), BF16)
                 + 2 * _nbytes((tm, tn), F32))
            + _nbytes((tm, k), BF16) + 4 * _nbytes((tm, tn), F32))
    return pl.pallas_call(
        _norm_matmul_res_kernel,
        grid=(t // tm, n // tn),
        in_specs=[pl.BlockSpec((tm, k), lambda i, j: (i, 0)),
                  pl.BlockSpec((tm, s), lambda i, j: (i, 0)),
                  pl.BlockSpec((1, k), lambda i, j: (0, 0)),
                  pl.BlockSpec((k, tn), lambda i, j: (0, j)),
                  pl.BlockSpec((tm, tn), lambda i, j: (i, j))],
        out_specs=pl.BlockSpec((tm, tn), lambda i, j: (i, j)),
        out_shape=jax.ShapeDtypeStruct((t, n), F32),
        scratch_shapes=[pltpu.VMEM((tm, k), BF16)],
        compiler_params=_params(("parallel", "arbitrary"), vmem),
        name="norm_matmul_res",
    )(yg, ssq, g.reshape(1, k), w, res)


def _conv_silu_kernel(u_ref, w_ref, b_ref, o_ref, pad_ref, *, rows):
    l, c = u_ref.shape[1], u_ref.shape[2]
    pad_ref[0:CONV_PAD_ROWS, :] = jnp.zeros((CONV_PAD_ROWS, c), F32)
    pad_ref[l + CONV_PAD_ROWS:l + 2 * CONV_PAD_ROWS, :] = jnp.zeros((CONV_PAD_ROWS, c), F32)
    pad_ref[CONV_PAD_ROWS:l + CONV_PAD_ROWS, :] = u_ref[0].astype(F32)
    half = SSD_CONV_W // 2

    for r0 in range(0, l, rows):
        acc = jnp.broadcast_to(b_ref[...], (rows, c))
        for k in range(SSD_CONV_W):
            lo = r0 + CONV_PAD_ROWS + k - half
            acc = acc + pad_ref[lo:lo + rows, :] * w_ref[k:k + 1, :]
        o_ref[0, r0:r0 + rows, :] = _silu(acc).astype(o_ref.dtype)


def _conv_silu(zxbc, conv_w, conv_b, *, batch, seq, tc, rows):
    cdim = conv_w.shape[1]
    col0 = D_INNER // tc
    vmem = 4 * _nbytes((seq, tc), F32) + _nbytes((seq + 2 * CONV_PAD_ROWS, tc), F32) \
        + 8 * _nbytes((rows, tc), F32)
    return pl.pallas_call(
        functools.partial(_conv_silu_kernel, rows=rows),
        grid=(batch, cdim // tc),
        in_specs=[pl.BlockSpec((1, seq, tc), lambda b, j: (b, 0, col0 + j)),
                  pl.BlockSpec((SSD_CONV_W, tc), lambda b, j: (0, j)),
                  pl.BlockSpec((1, tc), lambda b, j: (0, j))],
        out_specs=pl.BlockSpec((1, seq, tc), lambda b, j: (b, 0, j)),
        out_shape=jax.ShapeDtypeStruct((batch, seq, cdim), BF16),
        scratch_shapes=[pltpu.VMEM((seq + 2 * CONV_PAD_ROWS, tc), F32)],
        compiler_params=_params(("parallel", "parallel"), vmem),
        name="conv_silu",
    )(zxbc, conv_w, conv_b.reshape(1, cdim))


def _split_dot(v, m01):
    hi = v.astype(BF16)
    r1 = v - hi.astype(F32)
    mid = r1.astype(BF16)
    lo = (r1 - mid.astype(F32)).astype(BF16)
    out = jnp.dot(hi, m01, preferred_element_type=F32)
    out = out + jnp.dot(mid, m01, preferred_element_type=F32)
    return out + jnp.dot(lo, m01, preferred_element_type=F32)


def _softplus(v):
    return jnp.maximum(v, 0.0) + jnp.log1p(jnp.exp(-jnp.abs(v)))


def _log2(n):
    assert n & (n - 1) == 0
    return n.bit_length() - 1


def _ssd_prepare(reverse, dt_ref, bias_ref, alog_ref, r_cs, r_dt, t_cs, t_w, t_e, dec_ref):
    q, hg, p = SSD_CHUNK, SSD_HEADS_PER_GROUP, SSD_HEAD_DIM
    nc = dt_ref.shape[1] // q
    rows = nc * hg
    dt_all = _softplus(dt_ref[...] + bias_ref[...])
    dta_all = dt_all * (-jnp.exp(alog_ref[...]))
    stack = lambda v: jnp.concatenate([v[:, c * q:(c + 1) * q] for c in range(nc)], axis=0)
    dt_r = stack(dt_all)
    src = lax.broadcasted_iota(jnp.int32, (q, q), 0)
    dst = lax.broadcasted_iota(jnp.int32, (q, q), 1)
    cum01 = ((src >= dst) if reverse else (src <= dst)).astype(BF16)
    last_pos = 0 if reverse else q - 1
    cs = _split_dot(stack(dta_all), cum01)
    cs_last = cs[:, last_pos:last_pos + 1]
    cs_t = cs.T
    r_cs[...] = cs
    r_dt[...] = dt_r
    t_cs[...] = cs_t
    t_w[...] = (dt_r * jnp.exp(cs_last - cs)).T
    t_e[...] = jnp.exp(cs).T
    band_row = lax.shift_right_logical(lax.broadcasted_iota(jnp.int32, (rows, rows), 0), _log2(hg))
    band_col = lax.shift_right_logical(lax.broadcasted_iota(jnp.int32, (rows, rows), 1), _log2(hg))
    dec_sel = jnp.where(band_row == band_col, jnp.exp(cs_t[last_pos:last_pos + 1, :]), 0.0)
    head_of_row = lax.broadcasted_iota(jnp.int32, (rows, SSD_GROUP_DIM), 0) & (hg - 1)
    head_of_lane = lax.shift_right_logical(
        lax.broadcasted_iota(jnp.int32, (rows, SSD_GROUP_DIM), 1), _log2(p))
    dec_ref[...] = _split_dot(dec_sel, (head_of_row == head_of_lane).astype(BF16))


def _ssd_direction(reverse, x_ref, b_ref, c_ref, h_ref, r_cs, r_dt, t_cs, t_w, t_e, dec_ref, emit):
    q, hg, p, gd = SSD_CHUNK, SSD_HEADS_PER_GROUP, SSD_HEAD_DIM, SSD_GROUP_DIM
    nc = x_ref.shape[1] // q
    rows = nc * hg
    per_tile = V7X_MXU_WIDTH // p
    tgt = lax.broadcasted_iota(jnp.int32, (q, q), 0)
    src = lax.broadcasted_iota(jnp.int32, (q, q), 1)
    valid = (tgt <= src) if reverse else (tgt >= src)
    assert rows == V7X_LANES and 2 * p == V7X_LANES
    lane = lax.broadcasted_iota(jnp.int32, (q, V7X_LANES), 1)
    second_of_pair = lax.shift_right_logical(lane, _log2(p))
    tile_head = lax.shift_right_logical(
        lax.broadcasted_iota(jnp.int32, (q, V7X_MXU_WIDTH), 1), _log2(p))
    h_ref[...] = jnp.zeros(h_ref.shape, F32)

    def body(step, carry):
        chunk = (nc - 1 - step) if reverse else step
        r0 = pl.multiple_of(chunk * q, q)
        band = pl.ds(pl.multiple_of(chunk * hg, hg), hg)
        lane0 = chunk * hg

        def spread(t_ref):
            cols = t_ref[...]
            return jnp.concatenate(
                [jnp.take_along_axis(cols, lane0 + k + second_of_pair, axis=1)
                 for k in range(0, hg, 2)], axis=1)

        cs_all = t_cs[...]
        cs_rows = r_cs[band, :]
        dt_rows = r_dt[band, :]
        dec = dec_ref[band, :][0:1, :]

        x_bf = x_ref[0, pl.ds(r0, q), :]
        x_c = x_bf.astype(F32)
        b_bf = b_ref[0, pl.ds(r0, q), :]
        c_bf = c_ref[0, pl.ds(r0, q), :]
        cb = lax.dot_general(c_bf, b_bf, (((1,), (1,)), ((), ())),
                             preferred_element_type=F32)
        h_in = h_ref[...]
        y = jnp.dot(c_bf, h_in.astype(BF16), preferred_element_type=F32) * spread(t_e)
        wx = (x_c * spread(t_w)).astype(BF16)
        h_ref[...] = h_in * dec + lax.dot_general(b_bf, wx, (((0,), (0,)), ((), ())),
                                                  preferred_element_type=F32)

        y_tiles = []
        for t0 in range(0, hg, per_tile):
            ms, xs = [], []
            x_tile = x_bf[:, t0 * p:(t0 + per_tile) * p]
            for k in range(per_tile):
                h = t0 + k
                cs_tgt = jnp.take_along_axis(cs_all, jnp.broadcast_to(lane0 + h, lane.shape), axis=1)
                seg = cs_tgt - cs_rows[h:h + 1, :]
                lmat = jnp.exp(jnp.where(valid, seg, -jnp.inf))
                ms.append((cb * lmat * dt_rows[h:h + 1, :]).astype(BF16))
                xs.append(jnp.where(tile_head == k, x_tile, jnp.zeros_like(x_tile)))
            y_tiles.append(jnp.dot(jnp.concatenate(ms, axis=1), jnp.concatenate(xs, axis=0),
                                   preferred_element_type=F32))
        emit(r0, x_c, y + jnp.concatenate(y_tiles, axis=1))
        return carry

    lax.fori_loop(0, nc, body, 0, unroll=2)


def _ssd_scan_kernel(x_ref, b_ref, c_ref, z_ref, dtf_ref, dtb_ref, biasf_ref, biasb_ref,
                     alogf_ref, alogb_ref, dskip_ref, yg_ref, ssq_ref,
                     y_acc, h_ref, r_cs, r_dt, t_cs, t_w, t_e, dec_ref):
    q = SSD_CHUNK
    decay = (r_cs, r_dt, t_cs, t_w, t_e, dec_ref)

    def emit_fwd(r0, x_c, y):
        y_acc[pl.ds(r0, q), :] = y + x_c * dskip_ref[...]

    def emit_bwd(r0, x_c, y):
        yg = (y_acc[pl.ds(r0, q), :] + y) * _silu(z_ref[0, pl.ds(r0, q), :].astype(F32))
        yg_ref[0, pl.ds(r0, q), :] = yg.astype(yg_ref.dtype)
        sq = yg * yg
        folded = sq[:, 0:V7X_LANES]
        for k in range(1, SSD_GROUP_DIM // V7X_LANES):
            folded = folded + sq[:, k * V7X_LANES:(k + 1) * V7X_LANES]
        ssq_ref[0, pl.ds(r0, q), :] = folded

    _ssd_prepare(False, dtf_ref, biasf_ref, alogf_ref, *decay)
    _ssd_direction(False, x_ref, b_ref, c_ref, h_ref, *decay, emit_fwd)
    _ssd_prepare(True, dtb_ref, biasb_ref, alogb_ref, *decay)
    _ssd_direction(True, x_ref, b_ref, c_ref, h_ref, *decay, emit_bwd)


def _ssd_scan(xbc, zxbc, dt_t, dt_bias, a_log, d_skip, *, batch, seq):
    g, hg, gd, n = SSD_GROUPS, SSD_HEADS_PER_GROUP, SSD_GROUP_DIM, SSD_STATE
    bcol = D_INNER // n
    ccol = (D_INNER + SSD_BC_DIM) // n
    col = lambda v: v.reshape(2 * SSD_HEADS, 1)
    dskip_lanes = jnp.repeat(d_skip, SSD_HEAD_DIM).reshape(1, D_INNER)
    per_dir = lambda d: pl.BlockSpec((hg, 1), lambda b, j: (d * g + j, 0))
    rows = (seq // SSD_CHUNK) * hg
    vmem = (2 * (3 * _nbytes((seq, gd), F32) + 3 * _nbytes((seq, n), F32) + 2 * _nbytes((hg, seq), F32))
            + _nbytes((seq, gd), F32) + _nbytes((n, gd), F32) + 24 * _nbytes((SSD_CHUNK, gd), F32))
    return pl.pallas_call(
        _ssd_scan_kernel,
        grid=(batch, g),
        in_specs=[pl.BlockSpec((1, seq, gd), lambda b, j: (b, 0, j)),
                  pl.BlockSpec((1, seq, n), lambda b, j: (b, 0, bcol + j)),
                  pl.BlockSpec((1, seq, n), lambda b, j: (b, 0, ccol + j)),
                  pl.BlockSpec((1, seq, gd), lambda b, j: (b, 0, j)),
                  pl.BlockSpec((hg, seq), lambda b, j: (j, b)),
                  pl.BlockSpec((hg, seq), lambda b, j: (g + j, b)),
                  per_dir(0), per_dir(1), per_dir(0), per_dir(1),
                  pl.BlockSpec((1, gd), lambda b, j: (0, j))],
        out_specs=[pl.BlockSpec((1, seq, gd), lambda b, j: (b, 0, j)),
                   pl.BlockSpec((1, seq, V7X_LANES), lambda b, j: (b, 0, j))],
        out_shape=[jax.ShapeDtypeStruct((batch, seq, D_INNER), BF16),
                   jax.ShapeDtypeStruct((batch, seq, g * V7X_LANES), F32)],
        scratch_shapes=[pltpu.VMEM((seq, gd), F32), pltpu.VMEM((n, gd), F32)]
        + [pltpu.VMEM((rows, SSD_CHUNK), F32)] * 2 + [pltpu.VMEM((SSD_CHUNK, rows), F32)] * 3
        + [pltpu.VMEM((rows, gd), F32)],
        compiler_params=_params(("parallel", "parallel"), vmem),
        name="ssd_scan",
    )(xbc, xbc, xbc, zxbc, dt_t, dt_t, col(dt_bias), col(dt_bias), col(a_log), col(a_log),
      dskip_lanes)


def _ssd_mixer(x, norm_g, w_in_bf, conv_w, conv_b, dt_bias, a_log, d_skip, ssd_norm, w_out, *,
               batch, seq):
    w_dt_t = w_in_bf[:, SSD_ZXBC_DIM:].T
    zxbc, (w_out_bf,) = _rms_matmul(x, norm_g, w_in_bf, [w_out], n=SSD_ZXBC_DIM, tm=1024, tn=1024,
                                    out_dtype=BF16)
    dt_t = _rms_matmul_t(x, norm_g, w_dt_t, tm=1024)
    zxbc3 = zxbc.reshape(batch, seq, SSD_ZXBC_DIM)
    xbc = _conv_silu(zxbc3, conv_w, conv_b, batch=batch, seq=seq, tc=512, rows=64)
    yg, ssq = _ssd_scan(xbc, zxbc3, dt_t, dt_bias, a_log, d_skip, batch=batch, seq=seq)
    return _norm_matmul_res(yg.reshape(batch * seq, D_INNER), ssq.reshape(batch * seq, -1),
                            ssd_norm, w_out_bf, x, tm=1024, tn=512)


def _pool_kernel(u_ref, wg_ref, sc_ref, o_ref, pad_ref, *, rows):
    l, c = u_ref.shape[1], u_ref.shape[2]
    pad_ref[0:CONV_PAD_ROWS, :] = jnp.zeros((CONV_PAD_ROWS, c), F32)
    pad_ref[l + CONV_PAD_ROWS:l + 2 * CONV_PAD_ROWS, :] = jnp.zeros((CONV_PAD_ROWS, c), F32)
    pad_ref[CONV_PAD_ROWS:l + CONV_PAD_ROWS, :] = u_ref[0].astype(F32)

    for gi, win in enumerate(POOL_WINDOWS):
        half = win // 2

        @pl.when(pl.program_id(1) == gi)
        def _(half=half):
            def body(r, carry):
                r0 = pl.multiple_of(r * rows, rows)
                t = r0 + lax.broadcasted_iota(jnp.int32, (rows, 1), 0)
                cnt = jnp.minimum(t + half, l) - jnp.maximum(t - half, 0)
                inv = 1.0 / cnt.astype(F32)
                win = pad_ref[pl.ds(r0, rows + 2 * CONV_PAD_ROWS), :]
                s = win[CONV_PAD_ROWS - half:CONV_PAD_ROWS - half + rows, :]
                for k in range(1 - half, half):
                    s = s + win[CONV_PAD_ROWS + k:CONV_PAD_ROWS + k + rows, :]
                mix = (s * inv - win[CONV_PAD_ROWS:CONV_PAD_ROWS + rows, :]).astype(BF16)
                v = jnp.dot(mix, wg_ref[0], preferred_element_type=F32) * sc_ref[...]
                o_ref[0, pl.ds(r0, rows), :] = v.astype(o_ref.dtype)
                return carry

            lax.fori_loop(0, l // rows, body, 0)


def _pool(u, w_group, scale, *, batch, seq, rows):
    ng, gd = len(POOL_WINDOWS), POOL_GROUP_DIM
    vmem = (2 * (_nbytes((seq, gd), F32) + _nbytes((gd, gd), BF16) + _nbytes((seq, gd), BF16))
            + _nbytes((seq + 2 * CONV_PAD_ROWS, gd), F32) + 8 * _nbytes((rows, gd), F32))
    return pl.pallas_call(
        functools.partial(_pool_kernel, rows=rows),
        grid=(batch, ng),
        in_specs=[pl.BlockSpec((1, seq, gd), lambda b, j: (b, 0, j)),
                  pl.BlockSpec((1, gd, gd), lambda b, j: (j, 0, 0)),
                  pl.BlockSpec((1, gd), lambda b, j: (0, j))],
        out_specs=pl.BlockSpec((1, seq, gd), lambda b, j: (b, 0, j)),
        out_shape=jax.ShapeDtypeStruct((batch, seq, ng * gd), BF16),
        scratch_shapes=[pltpu.VMEM((seq + 2 * CONV_PAD_ROWS, gd), F32)],
        compiler_params=_params(("parallel", "parallel"), vmem),
        name="pool",
    )(u, w_group, scale.reshape(1, ng * gd))


def _pool_mixer(x, norm_g, w_in_bf, w_group_bf, scale, w_out_bf, *, batch, seq):
    u, _ = _rms_matmul(x, norm_g, w_in_bf, n=D_MODEL, tm=1024, tn=1024, out_dtype=F32)
    v = _pool(u.reshape(batch, seq, D_MODEL), w_group_bf, scale, batch=batch, seq=seq, rows=128)
    return _matmul_res(v.reshape(batch * seq, D_MODEL), w_out_bf, x, tm=1024, tn=512)


def _rmsnorm_kernel(x_ref, g_ref, o_ref):
    o_ref[...] = _rmsnorm_rows(x_ref[...], g_ref[...])


def _rmsnorm(x, g, *, tm):
    t, d = x.shape
    return pl.pallas_call(
        _rmsnorm_kernel,
        grid=(t // tm,),
        in_specs=[pl.BlockSpec((tm, d), lambda i: (i, 0)), pl.BlockSpec((1, d), lambda i: (0, 0))],
        out_specs=pl.BlockSpec((tm, d), lambda i: (i, 0)),
        out_shape=jax.ShapeDtypeStruct((t, d), F32),
        compiler_params=_params(("parallel",), 6 * _nbytes((tm, d), F32)),
        name="final_rmsnorm",
    )(x, g.reshape(1, d))


def kernel(x, ffn_norm, ffn_w_gate, ffn_w_up, ffn_w_down, mix_norm, ssd_w_in, ssd_conv_w,
           ssd_conv_b, ssd_dt_bias, ssd_a_log, ssd_d, ssd_norm, ssd_w_out, pool_w_in, pool_w_group,
           pool_scale, pool_w_out, final_norm):
    batch, seq, d = x.shape
    depth = ffn_norm.shape[0]
    h = x.reshape(batch * seq, d)

    n_ffn = 2 * depth
    ffn_stacks = (ffn_w_gate, ffn_w_up, ffn_w_down)
    ffn_w = [w[0, 0].astype(BF16) for w in ffn_stacks]
    pool_w_group2 = pool_w_group.reshape(pool_w_group.shape[0], -1, POOL_GROUP_DIM)

    def ffn(h, k, w_bf, mixer_raw):
        nxt = [(w, ((k + 1) // 2, (k + 1) % 2)) for w in ffn_stacks] if k + 1 < n_ffn else []
        h, cast = _ffn(h, ffn_norm[k // 2, k % 2], *w_bf, nxt + mixer_raw, tm=1024, tf=512)
        return h, cast[:len(nxt)], cast[len(nxt):]

    for i in range(depth):
        j = i // 2
        if i % 2 == 0:
            h, ffn_w, (w_in_bf,) = ffn(h, 2 * i, ffn_w, [(ssd_w_in, (j,))])
            h = _ssd_mixer(h, mix_norm[i], w_in_bf, ssd_conv_w[j], ssd_conv_b[j], ssd_dt_bias[j],
                           ssd_a_log[j], ssd_d[j], ssd_norm[j], (ssd_w_out, (j,)),
                           batch=batch, seq=seq)
        else:
            h, ffn_w, pool_w = ffn(h, 2 * i, ffn_w,
                                   [(pool_w_in, (j,)), (pool_w_group2, (j,)), (pool_w_out, (j,))])
            h = _pool_mixer(h, mix_norm[i], pool_w[0], pool_w[1].reshape(pool_w_group.shape[1:]),
                            pool_scale[j], pool_w[2], batch=batch, seq=seq)
        h, ffn_w, _ = ffn(h, 2 * i + 1, ffn_w, [])
    return _rmsnorm(h, final_norm, tm=512).reshape(batch, seq, d)
```

```python
import functools

import jax
import jax.numpy as jnp
from jax import lax
from jax.experimental import pallas as pl
from jax.experimental.pallas import tpu as pltpu

F32 = jnp.float32
BF16 = jnp.bfloat16

EPS = 1e-6
D_MODEL = 2048
D_FF = 5632
D_INNER = 2 * D_MODEL
SSD_HEAD_DIM = 64
SSD_HEADS = D_INNER // SSD_HEAD_DIM
SSD_GROUPS = 8
SSD_HEADS_PER_GROUP = SSD_HEADS // SSD_GROUPS
SSD_GROUP_DIM = SSD_HEADS_PER_GROUP * SSD_HEAD_DIM
SSD_STATE = 128
SSD_CONV_W = 5
SSD_CHUNK = 128
SSD_BC_DIM = SSD_GROUPS * SSD_STATE
SSD_ZXBC_DIM = 2 * D_INNER + 2 * SSD_BC_DIM
POOL_WINDOWS = (2, 4, 8, 16)
POOL_GROUP_DIM = D_MODEL // len(POOL_WINDOWS)

V7X_VMEM_BYTES = 64 * 1024 * 1024
V7X_LANES = 128
V7X_SUBLANES = 8
BF16_SUBLANES = 2 * V7X_SUBLANES
V7X_MXU_WIDTH = 256
CONV_PAD_ROWS = V7X_SUBLANES
NORM_ROW_CHUNK = 128

def _params(semantics, vmem_bytes):
    return pltpu.CompilerParams(dimension_semantics=semantics,
                                vmem_limit_bytes=min(int(vmem_bytes), V7X_VMEM_BYTES))


def _nbytes(shape, dtype):
    n = 1
    for s in shape:
        n *= s
    return n * jnp.dtype(dtype).itemsize


def _rmsnorm_rows(x, g):
    ms = jnp.mean(x * x, axis=-1, keepdims=True)
    return x * lax.rsqrt(ms + EPS) * g


def _silu(v):
    return v * jax.nn.sigmoid(v)


def _cast_slabs(arrays, ni, nj):
    in_specs, out_specs, out_shapes, vmem = [], [], [], 0
    for stack, lead in arrays:
        r, c = stack.shape[-2:]
        tiles = [(br, bc) for br in range(BF16_SUBLANES, r + 1, BF16_SUBLANES) if r % br == 0
                 for bc in range(V7X_LANES, c + 1, V7X_LANES) if c % bc == 0
                 and (r // br) * (c // bc) <= ni * nj]
        br, bc = min(tiles, key=lambda t: t[0] * t[1])
        ncb, nb = c // bc, (r // br) * (c // bc)

        def tile_index(i, j, ncb=ncb, nb=nb):
            blk = jnp.minimum(i * nj + j, nb - 1)
            return blk // ncb, blk % ncb

        in_specs.append(pl.BlockSpec((None,) * len(lead) + (br, bc),
                                     lambda i, j, lead=lead, f=tile_index: lead + f(i, j)))
        out_specs.append(pl.BlockSpec((br, bc), tile_index))
        out_shapes.append(jax.ShapeDtypeStruct((r, c), BF16))
        vmem += 2 * (_nbytes((br, bc), F32) + _nbytes((br, bc), BF16))
    return in_specs, out_specs, out_shapes, vmem


def _run_casts(cast_in, cast_out):
    for src, dst in zip(cast_in, cast_out):
        dst[...] = src[...].astype(BF16)


def _rms_matmul_kernel(x_ref, g_ref, w_ref, *rest, n_cast):
    cast_in, o_ref = rest[:n_cast], rest[n_cast]
    cast_out, xn_ref = rest[n_cast + 1:2 * n_cast + 1], rest[2 * n_cast + 1]

    @pl.when(pl.program_id(1) == 0)
    def _():
        xn_ref[...] = _rmsnorm_rows(x_ref[...], g_ref[...]).astype(BF16)

    o_ref[...] = jnp.dot(xn_ref[...], w_ref[...], preferred_element_type=F32).astype(o_ref.dtype)
    _run_casts(cast_in, cast_out)


def _rms_matmul(x, g, w, cast_next=(), *, n, tm, tn, out_dtype):
    t, d = x.shape
    ni, nj = t // tm, n // tn
    cast_in, cast_out, cast_shapes, cast_vmem = _cast_slabs(cast_next, ni, nj)
    vmem = (2 * (_nbytes((tm, d), F32) + _nbytes((d, tn), BF16) + _nbytes((tm, tn), out_dtype))
            + _nbytes((tm, d), BF16) + _nbytes((tm, d), F32) + _nbytes((tm, tn), F32) + cast_vmem)
    outs = pl.pallas_call(
        functools.partial(_rms_matmul_kernel, n_cast=len(cast_next)),
        grid=(ni, nj),
        in_specs=[pl.BlockSpec((tm, d), lambda i, j: (i, 0)),
                  pl.BlockSpec((1, d), lambda i, j: (0, 0)),
                  pl.BlockSpec((d, tn), lambda i, j: (0, j))] + cast_in,
        out_specs=[pl.BlockSpec((tm, tn), lambda i, j: (i, j))] + cast_out,
        out_shape=[jax.ShapeDtypeStruct((t, n), out_dtype)] + cast_shapes,
        scratch_shapes=[pltpu.VMEM((tm, d), BF16)],
        compiler_params=_params(("parallel", "arbitrary"), vmem),
        name="rms_matmul",
    )(x, g.reshape(1, d), w, *[stack for stack, _ in cast_next])
    return outs[0], outs[1:]


def _rms_matmul_t_kernel(x_ref, g_ref, wt_ref, o_ref):
    xn = _rmsnorm_rows(x_ref[...], g_ref[...]).astype(BF16)
    o_ref[...] = lax.dot_general(wt_ref[...], xn, (((1,), (1,)), ((), ())),
                                 preferred_element_type=F32)


def _rms_matmul_t(x, g, wt, *, tm):
    t, d = x.shape
    n = wt.shape[0]
    vmem = (2 * (_nbytes((tm, d), F32) + _nbytes((n, d), BF16) + _nbytes((n, tm), F32))
            + 2 * _nbytes((tm, d), F32))
    return pl.pallas_call(
        _rms_matmul_t_kernel,
        grid=(t // tm,),
        in_specs=[pl.BlockSpec((tm, d), lambda i: (i, 0)),
                  pl.BlockSpec((1, d), lambda i: (0, 0)),
                  pl.BlockSpec((n, d), lambda i: (0, 0))],
        out_specs=pl.BlockSpec((n, tm), lambda i: (0, i)),
        out_shape=jax.ShapeDtypeStruct((n, t), F32),
        compiler_params=_params(("parallel",), vmem),
        name="rms_matmul_t",
    )(x, g.reshape(1, d), wt)


def _ffn_kernel(x_ref, g_ref, wg_ref, wu_ref, wd_ref, *rest, n_cast):
    cast_in, o_ref = rest[:n_cast], rest[n_cast]
    cast_out, xn_ref = rest[n_cast + 1:2 * n_cast + 1], rest[2 * n_cast + 1]

    @pl.when(pl.program_id(1) == 0)
    def _():
        x = x_ref[...]
        xn_ref[...] = _rmsnorm_rows(x, g_ref[...]).astype(BF16)
        o_ref[...] = x

    xn = xn_ref[...]
    gate = jnp.dot(xn, wg_ref[...], preferred_element_type=F32)
    up = jnp.dot(xn, wu_ref[...], preferred_element_type=F32)
    hidden = (_silu(gate) * up * 0.5).astype(BF16)
    o_ref[...] += jnp.dot(hidden, wd_ref[...], preferred_element_type=F32)
    _run_casts(cast_in, cast_out)


def _ffn(x, g, wg, wu, wd, cast_next, *, tm, tf):
    t, d = x.shape
    f = wg.shape[1]
    ni, nj = t // tm, f // tf
    cast_in, cast_out, cast_shapes, cast_vmem = _cast_slabs(cast_next, ni, nj)
    vmem = (2 * (2 * _nbytes((tm, d), F32) + 2 * _nbytes((d, tf), BF16) + _nbytes((tf, d), BF16))
            + _nbytes((tm, d), BF16) + 4 * _nbytes((tm, tf), F32) + cast_vmem)
    outs = pl.pallas_call(
        functools.partial(_ffn_kernel, n_cast=len(cast_next)),
        grid=(ni, nj),
        in_specs=[pl.BlockSpec((tm, d), lambda i, j: (i, 0)),
                  pl.BlockSpec((1, d), lambda i, j: (0, 0)),
                  pl.BlockSpec((d, tf), lambda i, j: (0, j)),
                  pl.BlockSpec((d, tf), lambda i, j: (0, j)),
                  pl.BlockSpec((tf, d), lambda i, j: (j, 0))] + cast_in,
        out_specs=[pl.BlockSpec((tm, d), lambda i, j: (i, 0))] + cast_out,
        out_shape=[jax.ShapeDtypeStruct((t, d), F32)] + cast_shapes,
        scratch_shapes=[pltpu.VMEM((tm, d), BF16)],
        compiler_params=_params(("parallel", "arbitrary"), vmem),
        name="ffn",
    )(x, g.reshape(1, d), wg, wu, wd, *[stack for stack, _ in cast_next])
    return outs[0], outs[1:]


def _matmul_res_kernel(a_ref, w_ref, r_ref, o_ref):
    o_ref[...] = r_ref[...] + jnp.dot(a_ref[...], w_ref[...], preferred_element_type=F32)


def _matmul_res(a, w, res, *, tm, tn):
    t, k = a.shape
    n = w.shape[1]
    vmem = 2 * (_nbytes((tm, k), a.dtype) + _nbytes((k, tn), BF16) + 2 * _nbytes((tm, tn), F32)) \
        + _nbytes((tm, tn), F32)
    return pl.pallas_call(
        _matmul_res_kernel,
        grid=(t // tm, n // tn),
        in_specs=[pl.BlockSpec((tm, k), lambda i, j: (i, 0)),
                  pl.BlockSpec((k, tn), lambda i, j: (0, j)),
                  pl.BlockSpec((tm, tn), lambda i, j: (i, j))],
        out_specs=pl.BlockSpec((tm, tn), lambda i, j: (i, j)),
        out_shape=jax.ShapeDtypeStruct((t, n), F32),
        compiler_params=_params(("parallel", "arbitrary"), vmem),
        name="matmul_res",
    )(a, w, res)


def _norm_matmul_res_kernel(yg_ref, ssq_ref, g_ref, w_ref, r_ref, o_ref, an_ref):
    @pl.when(pl.program_id(1) == 0)
    def _():
        rc = NORM_ROW_CHUNK

        def body(r, carry):
            rows = pl.ds(pl.multiple_of(r * rc, rc), rc)
            ms = jnp.sum(ssq_ref[rows, :], axis=-1, keepdims=True) * (1.0 / yg_ref.shape[-1])
            an_ref[rows, :] = (yg_ref[rows, :].astype(F32) * lax.rsqrt(ms + EPS)
                               * g_ref[...]).astype(BF16)
            return carry

        lax.fori_loop(0, yg_ref.shape[0] // rc, body, 0)

    o_ref[...] = r_ref[...] + jnp.dot(an_ref[...], w_ref[...], preferred_element_type=F32)


def _norm_matmul_res(yg, ssq, g, w, res, *, tm, tn):
    t, k = yg.shape
    n = w.shape[1]
    s = ssq.shape[1]
    vmem = (2 * (_nbytes((tm, k), yg.dtype) + _nbytes((tm, s), F32) + _nbytes((k, tn), BF16)
                 + 2 * _nbytes((tm, tn), F32))
            + _nbytes((tm, k), BF16) + 4 * _nbytes((tm, tn), F32))
    return pl.pallas_call(
        _norm_matmul_res_kernel,
        grid=(t // tm, n // tn),
        in_specs=[pl.BlockSpec((tm, k), lambda i, j: (i, 0)),
                  pl.BlockSpec((tm, s), lambda i, j: (i, 0)),
                  pl.BlockSpec((1, k), lambda i, j: (0, 0)),
                  pl.BlockSpec((k, tn), lambda i, j: (0, j)),
                  pl.BlockSpec((tm, tn), lambda i, j: (i, j))],
        out_specs=pl.BlockSpec((tm, tn), lambda i, j: (i, j)),
        out_shape=jax.ShapeDtypeStruct((t, n), F32),
        scratch_shapes=[pltpu.VMEM((tm, k), BF16)],
        compiler_params=_params(("parallel", "arbitrary"), vmem),
        name="norm_matmul_res",
    )(yg, ssq, g.reshape(1, k), w, res)


def _inproj_conv_kernel(xn_ref, w_ref, cw_ref, cb_ref, *rest, n_cast, z_tiles, mm_rows, conv_rows):
    cast_in, o_ref = rest[:n_cast], rest[n_cast]
    cast_out = rest[n_cast + 1:2 * n_cast + 1]
    stage_ref = rest[2 * n_cast + 1]
    l, tn = o_ref.shape[1], o_ref.shape[2]
    j = pl.program_id(1)
    half = SSD_CONV_W // 2

    @pl.when(j < z_tiles)
    def _():
        o_ref[0] = jnp.dot(xn_ref[0], w_ref[...], preferred_element_type=F32).astype(o_ref.dtype)

    @pl.when(j >= z_tiles)
    def _():
        stage_ref[0:CONV_PAD_ROWS, :] = jnp.zeros((CONV_PAD_ROWS, tn), F32)
        stage_ref[l + CONV_PAD_ROWS:l + 2 * CONV_PAD_ROWS, :] = jnp.zeros((CONV_PAD_ROWS, tn), F32)
        n_mm = l // mm_rows
        for c in range(n_mm + 1):
            if c < n_mm:
                stage_ref[CONV_PAD_ROWS + c * mm_rows:CONV_PAD_ROWS + (c + 1) * mm_rows, :] = jnp.dot(
                    xn_ref[0, c * mm_rows:(c + 1) * mm_rows, :], w_ref[...],
                    preferred_element_type=F32)
            if c > 0:
                for r0 in range((c - 1) * mm_rows, c * mm_rows, conv_rows):
                    acc = jnp.broadcast_to(cb_ref[...], (conv_rows, tn))
                    for k in range(SSD_CONV_W):
                        lo = r0 + CONV_PAD_ROWS + k - half
                        acc = acc + stage_ref[lo:lo + conv_rows, :] * cw_ref[k:k + 1, :]
                    o_ref[0, r0:r0 + conv_rows, :] = _silu(acc).astype(o_ref.dtype)

    _run_casts(cast_in, cast_out)


def _inproj_conv(xn, w, conv_w, conv_b, cast_next, *, batch, seq, tn, mm_rows, conv_rows):
    d = xn.shape[1]
    z_tiles, nj = D_INNER // tn, SSD_ZXBC_DIM // tn
    cdim = conv_w.shape[1]
    conv_tile = lambda b, j: (0, jnp.maximum(j - z_tiles, 0))
    cast_in, cast_out, cast_shapes, cast_vmem = _cast_slabs(cast_next, batch, nj)
    stage_shape = (seq + 2 * CONV_PAD_ROWS, tn)
    vmem = (2 * (_nbytes((seq, d), BF16) + _nbytes((d, tn), BF16) + _nbytes((seq, tn), BF16))
            + _nbytes(stage_shape, F32) + 2 * _nbytes((seq, tn), F32) + cast_vmem)
    outs = pl.pallas_call(
        functools.partial(_inproj_conv_kernel, n_cast=len(cast_next), z_tiles=z_tiles,
                          mm_rows=mm_rows, conv_rows=conv_rows),
        grid=(batch, nj),
        in_specs=[pl.BlockSpec((1, seq, d), lambda b, j: (b, 0, 0)),
                  pl.BlockSpec((d, tn), lambda b, j: (0, j)),
                  pl.BlockSpec((SSD_CONV_W, tn), conv_tile),
                  pl.BlockSpec((1, tn), conv_tile)] + cast_in,
        out_specs=[pl.BlockSpec((1, seq, tn), lambda b, j: (b, 0, j))] + cast_out,
        out_shape=[jax.ShapeDtypeStruct((batch, seq, SSD_ZXBC_DIM), BF16)] + cast_shapes,
        scratch_shapes=[pltpu.VMEM(stage_shape, F32)],
        compiler_params=_params(("parallel", "parallel"), vmem),
        name="inproj_conv",
    )(xn.reshape(batch, seq, d), w, conv_w, conv_b.reshape(1, cdim),
      *[stack for stack, _ in cast_next])
    return outs[0], outs[1:]


def _split_dot(v, m01):
    hi = v.astype(BF16)
    r1 = v - hi.astype(F32)
    mid = r1.astype(BF16)
    lo = (r1 - mid.astype(F32)).astype(BF16)
    out = jnp.dot(hi, m01, preferred_element_type=F32)
    out = out + jnp.dot(mid, m01, preferred_element_type=F32)
    return out + jnp.dot(lo, m01, preferred_element_type=F32)


def _softplus(v):
    return jnp.maximum(v, 0.0) + jnp.log1p(jnp.exp(-jnp.abs(v)))


def _log2(n):
    assert n & (n - 1) == 0
    return n.bit_length() - 1


def _ssd_prepare(reverse, dt_ref, bias_ref, alog_ref, r_cs, r_dt, t_cs, t_w, t_e, dec_ref):
    q, hg, p = SSD_CHUNK, SSD_HEADS_PER_GROUP, SSD_HEAD_DIM
    nc = dt_ref.shape[1] // q
    rows = nc * hg
    dt_all = _softplus(dt_ref[...] + bias_ref[...])
    dta_all = dt_all * (-jnp.exp(alog_ref[...]))
    stack = lambda v: jnp.concatenate([v[:, c * q:(c + 1) * q] for c in range(nc)], axis=0)
    dt_r = stack(dt_all)
    src = lax.broadcasted_iota(jnp.int32, (q, q), 0)
    dst = lax.broadcasted_iota(jnp.int32, (q, q), 1)
    cum01 = ((src >= dst) if reverse else (src <= dst)).astype(BF16)
    last_pos = 0 if reverse else q - 1
    cs = _split_dot(stack(dta_all), cum01)
    cs_last = cs[:, last_pos:last_pos + 1]
    cs_t = cs.T
    r_cs[...] = cs
    r_dt[...] = dt_r
    t_cs[...] = cs_t
    t_w[...] = (dt_r * jnp.exp(cs_last - cs)).T
    t_e[...] = jnp.exp(cs).T
    band_row = lax.shift_right_logical(lax.broadcasted_iota(jnp.int32, (rows, rows), 0), _log2(hg))
    band_col = lax.shift_right_logical(lax.broadcasted_iota(jnp.int32, (rows, rows), 1), _log2(hg))
    dec_sel = jnp.where(band_row == band_col, jnp.exp(cs_t[last_pos:last_pos + 1, :]), 0.0)
    head_of_row = lax.broadcasted_iota(jnp.int32, (rows, SSD_GROUP_DIM), 0) & (hg - 1)
    head_of_lane = lax.shift_right_logical(
        lax.broadcasted_iota(jnp.int32, (rows, SSD_GROUP_DIM), 1), _log2(p))
    dec_ref[...] = _split_dot(dec_sel, (head_of_row == head_of_lane).astype(BF16))


def _ssd_direction(reverse, x_ref, b_ref, c_ref, h_ref, r_cs, r_dt, t_cs, t_w, t_e, dec_ref, emit):
    q, hg, p, gd = SSD_CHUNK, SSD_HEADS_PER_GROUP, SSD_HEAD_DIM, SSD_GROUP_DIM
    nc = x_ref.shape[1] // q
    rows = nc * hg
    per_tile = V7X_MXU_WIDTH // p
    tgt = lax.broadcasted_iota(jnp.int32, (q, q), 0)
    src = lax.broadcasted_iota(jnp.int32, (q, q), 1)
    valid = (tgt <= src) if reverse else (tgt >= src)
    assert rows == V7X_LANES and 2 * p == V7X_LANES
    lane = lax.broadcasted_iota(jnp.int32, (q, V7X_LANES), 1)
    second_of_pair = lax.shift_right_logical(lane, _log2(p))
    tile_head = lax.shift_right_logical(
        lax.broadcasted_iota(jnp.int32, (q, V7X_MXU_WIDTH), 1), _log2(p))
    h_ref[...] = jnp.zeros(h_ref.shape, F32)

    def body(step, carry):
        chunk = (nc - 1 - step) if reverse else step
        r0 = pl.multiple_of(chunk * q, q)
        band = pl.ds(pl.multiple_of(chunk * hg, hg), hg)
        lane0 = chunk * hg

        def spread(t_ref):
            cols = t_ref[...]
            return jnp.concatenate(
                [jnp.take_along_axis(cols, lane0 + k + second_of_pair, axis=1)
                 for k in range(0, hg, 2)], axis=1)

        cs_all = t_cs[...]
        cs_rows = r_cs[band, :]
        dt_rows = r_dt[band, :]
        dec = dec_ref[band, :][0:1, :]

        x_bf = x_ref[0, pl.ds(r0, q), :]
        x_c = x_bf.astype(F32)
        b_bf = b_ref[0, pl.ds(r0, q), :]
        c_bf = c_ref[0, pl.ds(r0, q), :]
        cb = lax.dot_general(c_bf, b_bf, (((1,), (1,)), ((), ())),
                             preferred_element_type=F32)
        h_in = h_ref[...]
        y = jnp.dot(c_bf, h_in.astype(BF16), preferred_element_type=F32) * spread(t_e)
        wx = (x_c * spread(t_w)).astype(BF16)
        h_ref[...] = h_in * dec + lax.dot_general(b_bf, wx, (((0,), (0,)), ((), ())),
                                                  preferred_element_type=F32)

        y_tiles = []
        for t0 in range(0, hg, per_tile):
            ms, xs = [], []
            x_tile = x_bf[:, t0 * p:(t0 + per_tile) * p]
            for k in range(per_tile):
                h = t0 + k
                cs_tgt = jnp.take_along_axis(cs_all, jnp.broadcast_to(lane0 + h, lane.shape), axis=1)
                seg = cs_tgt - cs_rows[h:h + 1, :]
                lmat = jnp.exp(jnp.where(valid, seg, -jnp.inf))
                ms.append((cb * lmat * dt_rows[h:h + 1, :]).astype(BF16))
                xs.append(jnp.where(tile_head == k, x_tile, jnp.zeros_like(x_tile)))
            y_tiles.append(jnp.dot(jnp.concatenate(ms, axis=1), jnp.concatenate(xs, axis=0),
                                   preferred_element_type=F32))
        emit(r0, x_c, y + jnp.concatenate(y_tiles, axis=1))
        return carry

    lax.fori_loop(0, nc, body, 0, unroll=8)


def _ssd_scan_kernel(x_ref, b_ref, c_ref, z_ref, dtf_ref, dtb_ref, biasf_ref, biasb_ref,
                     alogf_ref, alogb_ref, dskip_ref, yg_ref, ssq_ref,
                     y_acc, h_ref, r_cs, r_dt, t_cs, t_w, t_e, dec_ref):
    q = SSD_CHUNK
    decay = (r_cs, r_dt, t_cs, t_w, t_e, dec_ref)

    def emit_fwd(r0, x_c, y):
        y_acc[pl.ds(r0, q), :] = y + x_c * dskip_ref[...]

    def emit_bwd(r0, x_c, y):
        yg = (y_acc[pl.ds(r0, q), :] + y) * _silu(z_ref[0, pl.ds(r0, q), :].astype(F32))
        yg_ref[0, pl.ds(r0, q), :] = yg.astype(yg_ref.dtype)
        sq = yg * yg
        folded = sq[:, 0:V7X_LANES]
        for k in range(1, SSD_GROUP_DIM // V7X_LANES):
            folded = folded + sq[:, k * V7X_LANES:(k + 1) * V7X_LANES]
        ssq_ref[0, pl.ds(r0, q), :] = folded

    _ssd_prepare(False, dtf_ref, biasf_ref, alogf_ref, *decay)
    _ssd_direction(False, x_ref, b_ref, c_ref, h_ref, *decay, emit_fwd)
    _ssd_prepare(True, dtb_ref, biasb_ref, alogb_ref, *decay)
    _ssd_direction(True, x_ref, b_ref, c_ref, h_ref, *decay, emit_bwd)


def _ssd_scan(zxbc, dt_t, dt_bias, a_log, d_skip, *, batch, seq):
    g, hg, gd, n = SSD_GROUPS, SSD_HEADS_PER_GROUP, SSD_GROUP_DIM, SSD_STATE
    xcol = D_INNER // gd
    bcol = 2 * D_INNER // n
    ccol = (2 * D_INNER + SSD_BC_DIM) // n
    col = lambda v: v.reshape(2 * SSD_HEADS, 1)
    dskip_lanes = jnp.repeat(d_skip, SSD_HEAD_DIM).reshape(1, D_INNER)
    per_dir = lambda d: pl.BlockSpec((hg, 1), lambda b, j: (d * g + j, 0))
    rows = (seq // SSD_CHUNK) * hg
    vmem = (2 * (3 * _nbytes((seq, gd), F32) + 3 * _nbytes((seq, n), F32) + 2 * _nbytes((hg, seq), F32))
            + _nbytes((seq, gd), F32) + _nbytes((n, gd), F32) + 24 * _nbytes((SSD_CHUNK, gd), F32))
    return pl.pallas_call(
        _ssd_scan_kernel,
        grid=(batch, g),
        in_specs=[pl.BlockSpec((1, seq, gd), lambda b, j: (b, 0, xcol + j)),
                  pl.BlockSpec((1, seq, n), lambda b, j: (b, 0, bcol + j)),
                  pl.BlockSpec((1, seq, n), lambda b, j: (b, 0, ccol + j)),
                  pl.BlockSpec((1, seq, gd), lambda b, j: (b, 0, j)),
                  pl.BlockSpec((hg, seq), lambda b, j: (j, b)),
                  pl.BlockSpec((hg, seq), lambda b, j: (g + j, b)),
                  per_dir(0), per_dir(1), per_dir(0), per_dir(1),
                  pl.BlockSpec((1, gd), lambda b, j: (0, j))],
        out_specs=[pl.BlockSpec((1, seq, gd), lambda b, j: (b, 0, j)),
                   pl.BlockSpec((1, seq, V7X_LANES), lambda b, j: (b, 0, j))],
        out_shape=[jax.ShapeDtypeStruct((batch, seq, D_INNER), BF16),
                   jax.ShapeDtypeStruct((batch, seq, g * V7X_LANES), F32)],
        scratch_shapes=[pltpu.VMEM((seq, gd), F32), pltpu.VMEM((n, gd), F32)]
        + [pltpu.VMEM((rows, SSD_CHUNK), F32)] * 2 + [pltpu.VMEM((SSD_CHUNK, rows), F32)] * 3
        + [pltpu.VMEM((rows, gd), F32)],
        compiler_params=_params(("parallel", "parallel"), vmem),
        name="ssd_scan",
    )(zxbc, zxbc, zxbc, zxbc, dt_t, dt_t, col(dt_bias), col(dt_bias), col(a_log), col(a_log),
      dskip_lanes)


def _ssd_mixer(x, norm_g, w_in_bf, conv_w, conv_b, dt_bias, a_log, d_skip, ssd_norm, w_out, *,
               batch, seq):
    w_dt_t = w_in_bf[:, SSD_ZXBC_DIM:].T
    xn = _rmsnorm(x, norm_g, tm=512, out_dtype=BF16)
    zxbc, (w_out_bf,) = _inproj_conv(xn, w_in_bf, conv_w, conv_b, [w_out], batch=batch, seq=seq,
                                     tn=512, mm_rows=512, conv_rows=128)
    dt_t = _rms_matmul_t(x, norm_g, w_dt_t, tm=1024)
    yg, ssq = _ssd_scan(zxbc, dt_t, dt_bias, a_log, d_skip, batch=batch, seq=seq)
    return _norm_matmul_res(yg.reshape(batch * seq, D_INNER), ssq.reshape(batch * seq, -1),
                            ssd_norm, w_out_bf, x, tm=1024, tn=512)


def _pool_kernel(u_ref, wg_ref, sc_ref, o_ref, pad_ref, *, rows):
    l, c = u_ref.shape[1], u_ref.shape[2]
    pad_ref[0:CONV_PAD_ROWS, :] = jnp.zeros((CONV_PAD_ROWS, c), F32)
    pad_ref[l + CONV_PAD_ROWS:l + 2 * CONV_PAD_ROWS, :] = jnp.zeros((CONV_PAD_ROWS, c), F32)
    pad_ref[CONV_PAD_ROWS:l + CONV_PAD_ROWS, :] = u_ref[0].astype(F32)

    for gi, win in enumerate(POOL_WINDOWS):
        half = win // 2

        @pl.when(pl.program_id(1) == gi)
        def _(half=half):
            def body(r, carry):
                r0 = pl.multiple_of(r * rows, rows)
                t = r0 + lax.broadcasted_iota(jnp.int32, (rows, 1), 0)
                cnt = jnp.minimum(t + half, l) - jnp.maximum(t - half, 0)
                inv = 1.0 / cnt.astype(F32)
                win = pad_ref[pl.ds(r0, rows + 2 * CONV_PAD_ROWS), :]
                s = win[CONV_PAD_ROWS - half:CONV_PAD_ROWS - half + rows, :]
                for k in range(1 - half, half):
                    s = s + win[CONV_PAD_ROWS + k:CONV_PAD_ROWS + k + rows, :]
                mix = (s * inv - win[CONV_PAD_ROWS:CONV_PAD_ROWS + rows, :]).astype(BF16)
                v = jnp.dot(mix, wg_ref[0], preferred_element_type=F32) * sc_ref[...]
                o_ref[0, pl.ds(r0, rows), :] = v.astype(o_ref.dtype)
                return carry

            lax.fori_loop(0, l // rows, body, 0)


def _pool(u, w_group, scale, *, batch, seq, rows):
    ng, gd = len(POOL_WINDOWS), POOL_GROUP_DIM
    vmem = (2 * (_nbytes((seq, gd), F32) + _nbytes((gd, gd), BF16) + _nbytes((seq, gd), BF16))
            + _nbytes((seq + 2 * CONV_PAD_ROWS, gd), F32) + 8 * _nbytes((rows, gd), F32))
    return pl.pallas_call(
        functools.partial(_pool_kernel, rows=rows),
        grid=(batch, ng),
        in_specs=[pl.BlockSpec((1, seq, gd), lambda b, j: (b, 0, j)),
                  pl.BlockSpec((1, gd, gd), lambda b, j: (j, 0, 0)),
                  pl.BlockSpec((1, gd), lambda b, j: (0, j))],
        out_specs=pl.BlockSpec((1, seq, gd), lambda b, j: (b, 0, j)),
        out_shape=jax.ShapeDtypeStruct((batch, seq, ng * gd), BF16),
        scratch_shapes=[pltpu.VMEM((seq + 2 * CONV_PAD_ROWS, gd), F32)],
        compiler_params=_params(("parallel", "parallel"), vmem),
        name="pool",
    )(u, w_group, scale.reshape(1, ng * gd))


def _pool_mixer(x, norm_g, w_in_bf, w_group_bf, scale, w_out_bf, *, batch, seq):
    u, _ = _rms_matmul(x, norm_g, w_in_bf, n=D_MODEL, tm=1024, tn=1024, out_dtype=F32)
    v = _pool(u.reshape(batch, seq, D_MODEL), w_group_bf, scale, batch=batch, seq=seq, rows=128)
    return _matmul_res(v.reshape(batch * seq, D_MODEL), w_out_bf, x, tm=1024, tn=512)


def _rmsnorm_kernel(x_ref, g_ref, o_ref):
    o_ref[...] = _rmsnorm_rows(x_ref[...], g_ref[...]).astype(o_ref.dtype)


def _rmsnorm(x, g, *, tm, out_dtype):
    t, d = x.shape
    return pl.pallas_call(
        _rmsnorm_kernel,
        grid=(t // tm,),
        in_specs=[pl.BlockSpec((tm, d), lambda i: (i, 0)), pl.BlockSpec((1, d), lambda i: (0, 0))],
        out_specs=pl.BlockSpec((tm, d), lambda i: (i, 0)),
        out_shape=jax.ShapeDtypeStruct((t, d), out_dtype),
        compiler_params=_params(("parallel",), 6 * _nbytes((tm, d), F32)),
        name="rmsnorm",
    )(x, g.reshape(1, d))


def kernel(x, ffn_norm, ffn_w_gate, ffn_w_up, ffn_w_down, mix_norm, ssd_w_in, ssd_conv_w,
           ssd_conv_b, ssd_dt_bias, ssd_a_log, ssd_d, ssd_norm, ssd_w_out, pool_w_in, pool_w_group,
           pool_scale, pool_w_out, final_norm):
    batch, seq, d = x.shape
    depth = ffn_norm.shape[0]
    h = x.reshape(batch * seq, d)

    n_ffn = 2 * depth
    ffn_stacks = (ffn_w_gate, ffn_w_up, ffn_w_down)
    ffn_w = [w[0, 0].astype(BF16) for w in ffn_stacks]
    pool_w_group2 = pool_w_group.reshape(pool_w_group.shape[0], -1, POOL_GROUP_DIM)

    def ffn(h, k, w_bf, mixer_raw):
        nxt = [(w, ((k + 1) // 2, (k + 1) % 2)) for w in ffn_stacks] if k + 1 < n_ffn else []
        h, cast = _ffn(h, ffn_norm[k // 2, k % 2], *w_bf, nxt + mixer_raw, tm=1024, tf=512)
        return h, cast[:len(nxt)], cast[len(nxt):]

    for i in range(depth):
        j = i // 2
        if i % 2 == 0:
            h, ffn_w, (w_in_bf,) = ffn(h, 2 * i, ffn_w, [(ssd_w_in, (j,))])
            h = _ssd_mixer(h, mix_norm[i], w_in_bf, ssd_conv_w[j], ssd_conv_b[j], ssd_dt_bias[j],
                           ssd_a_log[j], ssd_d[j], ssd_norm[j], (ssd_w_out, (j,)),
                           batch=batch, seq=seq)
        else:
            h, ffn_w, pool_w = ffn(h, 2 * i, ffn_w,
                                   [(pool_w_in, (j,)), (pool_w_group2, (j,)), (pool_w_out, (j,))])
            h = _pool_mixer(h, mix_norm[i], pool_w[0], pool_w[1].reshape(pool_w_group.shape[1:]),
                            pool_scale[j], pool_w[2], batch=batch, seq=seq)
        h, ffn_w, _ = ffn(h, 2 * i + 1, ffn_w, [])
    return _rmsnorm(h, final_norm, tm=512, out_dtype=F32).reshape(batch, seq, d)
```

```python
import functools

import jax
import jax.numpy as jnp
from jax import lax
from jax.experimental import pallas as pl
from jax.experimental.pallas import tpu as pltpu

F32 = jnp.float32
BF16 = jnp.bfloat16

EPS = 1e-6
D_MODEL = 2048
D_FF = 5632
D_INNER = 2 * D_MODEL
SSD_HEAD_DIM = 64
SSD_HEADS = D_INNER // SSD_HEAD_DIM
SSD_GROUPS = 8
SSD_HEADS_PER_GROUP = SSD_HEADS // SSD_GROUPS
SSD_GROUP_DIM = SSD_HEADS_PER_GROUP * SSD_HEAD_DIM
SSD_STATE = 128
SSD_CONV_W = 5
SSD_CHUNK = 128
SSD_BC_DIM = SSD_GROUPS * SSD_STATE
SSD_ZXBC_DIM = 2 * D_INNER + 2 * SSD_BC_DIM
POOL_WINDOWS = (2, 4, 8, 16)
POOL_GROUP_DIM = D_MODEL // len(POOL_WINDOWS)

V7X_VMEM_BYTES = 64 * 1024 * 1024
V7X_LANES = 128
V7X_SUBLANES = 8
BF16_SUBLANES = 2 * V7X_SUBLANES
V7X_MXU_WIDTH = 256
CONV_PAD_ROWS = V7X_SUBLANES
NORM_ROW_CHUNK = 128

def _params(semantics, vmem_bytes):
    return pltpu.CompilerParams(dimension_semantics=semantics,
                                vmem_limit_bytes=min(int(vmem_bytes), V7X_VMEM_BYTES))


def _nbytes(shape, dtype):
    n = 1
    for s in shape:
        n *= s
    return n * jnp.dtype(dtype).itemsize


def _rmsnorm_rows(x, g):
    ms = jnp.mean(x * x, axis=-1, keepdims=True)
    return x * lax.rsqrt(ms + EPS) * g


def _silu(v):
    return v * jax.nn.sigmoid(v)


def _cast_slabs(arrays, ni, nj):
    in_specs, out_specs, out_shapes, vmem = [], [], [], 0
    for stack, lead in arrays:
        r, c = stack.shape[-2:]
        tiles = [(br, bc) for br in range(BF16_SUBLANES, r + 1, BF16_SUBLANES) if r % br == 0
                 for bc in range(V7X_LANES, c + 1, V7X_LANES) if c % bc == 0
                 and (r // br) * (c // bc) <= ni * nj]
        br, bc = min(tiles, key=lambda t: t[0] * t[1])
        ncb, nb = c // bc, (r // br) * (c // bc)

        def tile_index(i, j, ncb=ncb, nb=nb):
            blk = jnp.minimum(i * nj + j, nb - 1)
            return blk // ncb, blk % ncb

        in_specs.append(pl.BlockSpec((None,) * len(lead) + (br, bc),
                                     lambda i, j, lead=lead, f=tile_index: lead + f(i, j)))
        out_specs.append(pl.BlockSpec((br, bc), tile_index))
        out_shapes.append(jax.ShapeDtypeStruct((r, c), BF16))
        vmem += 2 * (_nbytes((br, bc), F32) + _nbytes((br, bc), BF16))
    return in_specs, out_specs, out_shapes, vmem


def _run_casts(cast_in, cast_out):
    for src, dst in zip(cast_in, cast_out):
        dst[...] = src[...].astype(BF16)


def _rms_matmul_kernel(x_ref, g_ref, w_ref, *rest, n_cast):
    cast_in, o_ref = rest[:n_cast], rest[n_cast]
    cast_out, xn_ref = rest[n_cast + 1:2 * n_cast + 1], rest[2 * n_cast + 1]

    @pl.when(pl.program_id(1) == 0)
    def _():
        xn_ref[...] = _rmsnorm_rows(x_ref[...], g_ref[...]).astype(BF16)

    o_ref[...] = jnp.dot(xn_ref[...], w_ref[...], preferred_element_type=F32).astype(o_ref.dtype)
    _run_casts(cast_in, cast_out)


def _rms_matmul(x, g, w, cast_next=(), *, n, tm, tn, out_dtype):
    t, d = x.shape
    ni, nj = t // tm, n // tn
    cast_in, cast_out, cast_shapes, cast_vmem = _cast_slabs(cast_next, ni, nj)
    vmem = (2 * (_nbytes((tm, d), F32) + _nbytes((d, tn), BF16) + _nbytes((tm, tn), out_dtype))
            + _nbytes((tm, d), BF16) + _nbytes((tm, d), F32) + _nbytes((tm, tn), F32) + cast_vmem)
    outs = pl.pallas_call(
        functools.partial(_rms_matmul_kernel, n_cast=len(cast_next)),
        grid=(ni, nj),
        in_specs=[pl.BlockSpec((tm, d), lambda i, j: (i, 0)),
                  pl.BlockSpec((1, d), lambda i, j: (0, 0)),
                  pl.BlockSpec((d, tn), lambda i, j: (0, j))] + cast_in,
        out_specs=[pl.BlockSpec((tm, tn), lambda i, j: (i, j))] + cast_out,
        out_shape=[jax.ShapeDtypeStruct((t, n), out_dtype)] + cast_shapes,
        scratch_shapes=[pltpu.VMEM((tm, d), BF16)],
        compiler_params=_params(("parallel", "arbitrary"), vmem),
        name="rms_matmul",
    )(x, g.reshape(1, d), w, *[stack for stack, _ in cast_next])
    return outs[0], outs[1:]


def _rms_matmul_t_kernel(x_ref, g_ref, wt_ref, o_ref):
    xn = _rmsnorm_rows(x_ref[...], g_ref[...]).astype(BF16)
    o_ref[...] = lax.dot_general(wt_ref[...], xn, (((1,), (1,)), ((), ())),
                                 preferred_element_type=F32)


def _rms_matmul_t(x, g, wt, *, tm):
    t, d = x.shape
    n = wt.shape[0]
    vmem = (2 * (_nbytes((tm, d), F32) + _nbytes((n, d), BF16) + _nbytes((n, tm), F32))
            + 2 * _nbytes((tm, d), F32))
    return pl.pallas_call(
        _rms_matmul_t_kernel,
        grid=(t // tm,),
        in_specs=[pl.BlockSpec((tm, d), lambda i: (i, 0)),
                  pl.BlockSpec((1, d), lambda i: (0, 0)),
                  pl.BlockSpec((n, d), lambda i: (0, 0))],
        out_specs=pl.BlockSpec((n, tm), lambda i: (0, i)),
        out_shape=jax.ShapeDtypeStruct((n, t), F32),
        compiler_params=_params(("parallel",), vmem),
        name="rms_matmul_t",
    )(x, g.reshape(1, d), wt)


def _ffn_kernel(x_ref, g_ref, wg_ref, wu_ref, wd_ref, *rest, n_cast):
    cast_in, o_ref = rest[:n_cast], rest[n_cast]
    cast_out, xn_ref = rest[n_cast + 1:2 * n_cast + 1], rest[2 * n_cast + 1]

    @pl.when(pl.program_id(1) == 0)
    def _():
        x = x_ref[...]
        xn_ref[...] = _rmsnorm_rows(x, g_ref[...]).astype(BF16)
        o_ref[...] = x

    xn = xn_ref[...]
    gate = jnp.dot(xn, wg_ref[...], preferred_element_type=F32)
    up = jnp.dot(xn, wu_ref[...], preferred_element_type=F32)
    hidden = (_silu(gate) * up * 0.5).astype(BF16)
    o_ref[...] += jnp.dot(hidden, wd_ref[...], preferred_element_type=F32)
    _run_casts(cast_in, cast_out)


def _ffn(x, g, wg, wu, wd, cast_next, *, tm, tf):
    t, d = x.shape
    f = wg.shape[1]
    ni, nj = t // tm, f // tf
    cast_in, cast_out, cast_shapes, cast_vmem = _cast_slabs(cast_next, ni, nj)
    vmem = (2 * (2 * _nbytes((tm, d), F32) + 2 * _nbytes((d, tf), BF16) + _nbytes((tf, d), BF16))
            + _nbytes((tm, d), BF16) + 4 * _nbytes((tm, tf), F32) + cast_vmem)
    outs = pl.pallas_call(
        functools.partial(_ffn_kernel, n_cast=len(cast_next)),
        grid=(ni, nj),
        in_specs=[pl.BlockSpec((tm, d), lambda i, j: (i, 0)),
                  pl.BlockSpec((1, d), lambda i, j: (0, 0)),
                  pl.BlockSpec((d, tf), lambda i, j: (0, j)),
                  pl.BlockSpec((d, tf), lambda i, j: (0, j)),
                  pl.BlockSpec((tf, d), lambda i, j: (j, 0))] + cast_in,
        out_specs=[pl.BlockSpec((tm, d), lambda i, j: (i, 0))] + cast_out,
        out_shape=[jax.ShapeDtypeStruct((t, d), F32)] + cast_shapes,
        scratch_shapes=[pltpu.VMEM((tm, d), BF16)],
        compiler_params=_params(("parallel", "arbitrary"), vmem),
        name="ffn",
    )(x, g.reshape(1, d), wg, wu, wd, *[stack for stack, _ in cast_next])
    return outs[0], outs[1:]


def _matmul_res_kernel(a_ref, w_ref, r_ref, o_ref):
    o_ref[...] = r_ref[...] + jnp.dot(a_ref[...], w_ref[...], preferred_element_type=F32)


def _matmul_res(a, w, res, *, tm, tn):
    t, k = a.shape
    n = w.shape[1]
    vmem = 2 * (_nbytes((tm, k), a.dtype) + _nbytes((k, tn), BF16) + 2 * _nbytes((tm, tn), F32)) \
        + _nbytes((tm, tn), F32)
    return pl.pallas_call(
        _matmul_res_kernel,
        grid=(t // tm, n // tn),
        in_specs=[pl.BlockSpec((tm, k), lambda i, j: (i, 0)),
                  pl.BlockSpec((k, tn), lambda i, j: (0, j)),
                  pl.BlockSpec((tm, tn), lambda i, j: (i, j))],
        out_specs=pl.BlockSpec((tm, tn), lambda i, j: (i, j)),
        out_shape=jax.ShapeDtypeStruct((t, n), F32),
        compiler_params=_params(("parallel", "arbitrary"), vmem),
        name="matmul_res",
    )(a, w, res)


def _norm_matmul_res_kernel(yg_ref, ssq_ref, g_ref, w_ref, r_ref, o_ref, an_ref):
    @pl.when(pl.program_id(1) == 0)
    def _():
        rc = NORM_ROW_CHUNK

        def body(r, carry):
            rows = pl.ds(pl.multiple_of(r * rc, rc), rc)
            ms = jnp.sum(ssq_ref[rows, :], axis=-1, keepdims=True) * (1.0 / yg_ref.shape[-1])
            an_ref[rows, :] = (yg_ref[rows, :].astype(F32) * lax.rsqrt(ms + EPS)
                               * g_ref[...]).astype(BF16)
            return carry

        lax.fori_loop(0, yg_ref.shape[0] // rc, body, 0)

    o_ref[...] = r_ref[...] + jnp.dot(an_ref[...], w_ref[...], preferred_element_type=F32)


def _norm_matmul_res(yg, ssq, g, w, res, *, tm, tn):
    t, k = yg.shape
    n = w.shape[1]
    s = ssq.shape[1]
    w_bufs = 1 if tn == n else 2
    w_mode = dict(pipeline_mode=pl.Buffered(1)) if w_bufs == 1 else {}
    vmem = (2 * (_nbytes((tm, k), yg.dtype) + _nbytes((tm, s), F32) + 2 * _nbytes((tm, tn), F32))
            + w_bufs * _nbytes((k, tn), BF16)
            + _nbytes((tm, k), BF16) + 2 * _nbytes((tm, tn), F32))
    return pl.pallas_call(
        _norm_matmul_res_kernel,
        grid=(t // tm, n // tn),
        in_specs=[pl.BlockSpec((tm, k), lambda i, j: (i, 0)),
                  pl.BlockSpec((tm, s), lambda i, j: (i, 0)),
                  pl.BlockSpec((1, k), lambda i, j: (0, 0)),
                  pl.BlockSpec((k, tn), lambda i, j: (0, j), **w_mode),
                  pl.BlockSpec((tm, tn), lambda i, j: (i, j))],
        out_specs=pl.BlockSpec((tm, tn), lambda i, j: (i, j)),
        out_shape=jax.ShapeDtypeStruct((t, n), F32),
        scratch_shapes=[pltpu.VMEM((tm, k), BF16)],
        compiler_params=_params(("parallel", "arbitrary"), vmem),
        name="norm_matmul_res",
    )(yg, ssq, g.reshape(1, k), w, res)


def _inproj_conv_kernel(x_ref, g_ref, w_ref, cw_ref, cb_ref, *rest, n_cast, z_tiles, mm_rows,
                        conv_rows):
    cast_in, o_ref = rest[:n_cast], rest[n_cast]
    cast_out = rest[n_cast + 1:2 * n_cast + 1]
    xn_ref, stage_ref = rest[2 * n_cast + 1:]
    l, tn = o_ref.shape[1], o_ref.shape[2]
    j = pl.program_id(1)
    half = SSD_CONV_W // 2

    @pl.when(j == 0)
    def _():
        rc = NORM_ROW_CHUNK

        def body(r, carry):
            rows = pl.ds(pl.multiple_of(r * rc, rc), rc)
            xn_ref[rows, :] = _rmsnorm_rows(x_ref[0, rows, :], g_ref[...]).astype(BF16)
            return carry

        lax.fori_loop(0, l // rc, body, 0)

    @pl.when(j < z_tiles)
    def _():
        o_ref[0] = jnp.dot(xn_ref[...], w_ref[...], preferred_element_type=F32).astype(o_ref.dtype)

    @pl.when(j >= z_tiles)
    def _():
        stage_ref[0:CONV_PAD_ROWS, :] = jnp.zeros((CONV_PAD_ROWS, tn), F32)
        stage_ref[l + CONV_PAD_ROWS:l + 2 * CONV_PAD_ROWS, :] = jnp.zeros((CONV_PAD_ROWS, tn), F32)
        n_mm = l // mm_rows
        for c in range(n_mm + 1):
            if c < n_mm:
                stage_ref[CONV_PAD_ROWS + c * mm_rows:CONV_PAD_ROWS + (c + 1) * mm_rows, :] = jnp.dot(
                    xn_ref[c * mm_rows:(c + 1) * mm_rows, :], w_ref[...],
                    preferred_element_type=F32)
            if c > 0:
                for r0 in range((c - 1) * mm_rows, c * mm_rows, conv_rows):
                    acc = jnp.broadcast_to(cb_ref[...], (conv_rows, tn))
                    for k in range(SSD_CONV_W):
                        lo = r0 + CONV_PAD_ROWS + k - half
                        acc = acc + stage_ref[lo:lo + conv_rows, :] * cw_ref[k:k + 1, :]
                    o_ref[0, r0:r0 + conv_rows, :] = _silu(acc).astype(o_ref.dtype)

    _run_casts(cast_in, cast_out)


def _inproj_conv(x, g, w, conv_w, conv_b, cast_next, *, batch, seq, tn, mm_rows, conv_rows):
    d = x.shape[1]
    z_tiles, nj = D_INNER // tn, SSD_ZXBC_DIM // tn
    cdim = conv_w.shape[1]
    conv_tile = lambda b, j: (0, jnp.maximum(j - z_tiles, 0))
    cast_in, cast_out, cast_shapes, cast_vmem = _cast_slabs(cast_next, batch, nj)
    stage_shape = (seq + 2 * CONV_PAD_ROWS, tn)
    vmem = (2 * (_nbytes((seq, d), F32) + _nbytes((d, tn), BF16) + _nbytes((seq, tn), BF16))
            + _nbytes((seq, d), BF16) + _nbytes(stage_shape, F32) + 6 * _nbytes((mm_rows, tn), F32)
            + cast_vmem)
    outs = pl.pallas_call(
        functools.partial(_inproj_conv_kernel, n_cast=len(cast_next), z_tiles=z_tiles,
                          mm_rows=mm_rows, conv_rows=conv_rows),
        grid=(batch, nj),
        in_specs=[pl.BlockSpec((1, seq, d), lambda b, j: (b, 0, 0)),
                  pl.BlockSpec((1, d), lambda b, j: (0, 0)),
                  pl.BlockSpec((d, tn), lambda b, j: (0, j)),
                  pl.BlockSpec((SSD_CONV_W, tn), conv_tile),
                  pl.BlockSpec((1, tn), conv_tile)] + cast_in,
        out_specs=[pl.BlockSpec((1, seq, tn), lambda b, j: (b, 0, j))] + cast_out,
        out_shape=[jax.ShapeDtypeStruct((batch, seq, SSD_ZXBC_DIM), BF16)] + cast_shapes,
        scratch_shapes=[pltpu.VMEM((seq, d), BF16), pltpu.VMEM(stage_shape, F32)],
        compiler_params=_params(("parallel", "arbitrary"), vmem),
        name="inproj_conv",
    )(x.reshape(batch, seq, d), g.reshape(1, d), w, conv_w, conv_b.reshape(1, cdim),
      *[stack for stack, _ in cast_next])
    return outs[0], outs[1:]


def _split_dot(v, m01):
    hi = v.astype(BF16)
    r1 = v - hi.astype(F32)
    mid = r1.astype(BF16)
    lo = (r1 - mid.astype(F32)).astype(BF16)
    out = jnp.dot(hi, m01, preferred_element_type=F32)
    out = out + jnp.dot(mid, m01, preferred_element_type=F32)
    return out + jnp.dot(lo, m01, preferred_element_type=F32)


def _softplus(v):
    return jnp.maximum(v, 0.0) + jnp.log1p(jnp.exp(-jnp.abs(v)))


def _log2(n):
    assert n & (n - 1) == 0
    return n.bit_length() - 1


def _ssd_prepare(reverse, dt_ref, bias_ref, alog_ref, r_cs, r_dt, t_cs, t_w, t_e, dec_ref):
    q, hg, p = SSD_CHUNK, SSD_HEADS_PER_GROUP, SSD_HEAD_DIM
    nc = dt_ref.shape[1] // q
    rows = nc * hg
    dt_all = _softplus(dt_ref[...] + bias_ref[...])
    dta_all = dt_all * (-jnp.exp(alog_ref[...]))
    stack = lambda v: jnp.concatenate([v[:, c * q:(c + 1) * q] for c in range(nc)], axis=0)
    dt_r = stack(dt_all)
    src = lax.broadcasted_iota(jnp.int32, (q, q), 0)
    dst = lax.broadcasted_iota(jnp.int32, (q, q), 1)
    cum01 = ((src >= dst) if reverse else (src <= dst)).astype(BF16)
    last_pos = 0 if reverse else q - 1
    cs = _split_dot(stack(dta_all), cum01)
    cs_last = cs[:, last_pos:last_pos + 1]
    cs_t = cs.T
    r_cs[...] = cs
    r_dt[...] = dt_r
    t_cs[...] = cs_t
    t_w[...] = (dt_r * jnp.exp(cs_last - cs)).T
    t_e[...] = jnp.exp(cs).T
    band_row = lax.shift_right_logical(lax.broadcasted_iota(jnp.int32, (rows, rows), 0), _log2(hg))
    band_col = lax.shift_right_logical(lax.broadcasted_iota(jnp.int32, (rows, rows), 1), _log2(hg))
    dec_sel = jnp.where(band_row == band_col, jnp.exp(cs_t[last_pos:last_pos + 1, :]), 0.0)
    head_of_row = lax.broadcasted_iota(jnp.int32, (rows, SSD_GROUP_DIM), 0) & (hg - 1)
    head_of_lane = lax.shift_right_logical(
        lax.broadcasted_iota(jnp.int32, (rows, SSD_GROUP_DIM), 1), _log2(p))
    dec_ref[...] = _split_dot(dec_sel, (head_of_row == head_of_lane).astype(BF16))


def _ssd_direction(reverse, x_ref, b_ref, c_ref, h_ref, r_cs, r_dt, t_cs, t_w, t_e, dec_ref, emit):
    q, hg, p, gd = SSD_CHUNK, SSD_HEADS_PER_GROUP, SSD_HEAD_DIM, SSD_GROUP_DIM
    nc = x_ref.shape[1] // q
    rows = nc * hg
    per_tile = V7X_MXU_WIDTH // p
    tgt = lax.broadcasted_iota(jnp.int32, (q, q), 0)
    src = lax.broadcasted_iota(jnp.int32, (q, q), 1)
    valid = (tgt <= src) if reverse else (tgt >= src)
    assert rows == V7X_LANES and 2 * p == V7X_LANES
    lane = lax.broadcasted_iota(jnp.int32, (q, V7X_LANES), 1)
    second_of_pair = lax.shift_right_logical(lane, _log2(p))
    tile_head = lax.shift_right_logical(
        lax.broadcasted_iota(jnp.int32, (q, V7X_MXU_WIDTH), 1), _log2(p))
    h_ref[...] = jnp.zeros(h_ref.shape, F32)

    def body(step, carry):
        chunk = (nc - 1 - step) if reverse else step
        r0 = pl.multiple_of(chunk * q, q)
        band = pl.ds(pl.multiple_of(chunk * hg, hg), hg)
        lane0 = chunk * hg

        def spread(t_ref):
            cols = t_ref[...]
            return jnp.concatenate(
                [jnp.take_along_axis(cols, lane0 + k + second_of_pair, axis=1)
                 for k in range(0, hg, 2)], axis=1)

        cs_all = t_cs[...]
        cs_rows = r_cs[band, :]
        dt_rows = r_dt[band, :]
        dec = dec_ref[band, :][0:1, :]

        x_bf = x_ref[0, pl.ds(r0, q), :]
        x_c = x_bf.astype(F32)
        b_bf = b_ref[0, pl.ds(r0, q), :]
        c_bf = c_ref[0, pl.ds(r0, q), :]
        cb = lax.dot_general(c_bf, b_bf, (((1,), (1,)), ((), ())),
                             preferred_element_type=F32)
        h_in = h_ref[...]
        y = jnp.dot(c_bf, h_in.astype(BF16), preferred_element_type=F32) * spread(t_e)
        wx = (x_c * spread(t_w)).astype(BF16)
        h_ref[...] = h_in * dec + lax.dot_general(b_bf, wx, (((0,), (0,)), ((), ())),
                                                  preferred_element_type=F32)

        y_tiles = []
        for t0 in range(0, hg, per_tile):
            ms, xs = [], []
            x_tile = x_bf[:, t0 * p:(t0 + per_tile) * p]
            for k in range(per_tile):
                h = t0 + k
                cs_tgt = jnp.take_along_axis(cs_all, jnp.broadcast_to(lane0 + h, lane.shape), axis=1)
                seg = cs_tgt - cs_rows[h:h + 1, :]
                lmat = jnp.exp(jnp.where(valid, seg, -jnp.inf))
                ms.append((cb * lmat * dt_rows[h:h + 1, :]).astype(BF16))
                xs.append(jnp.where(tile_head == k, x_tile, jnp.zeros_like(x_tile)))
            y_tiles.append(jnp.dot(jnp.concatenate(ms, axis=1), jnp.concatenate(xs, axis=0),
                                   preferred_element_type=F32))
        emit(r0, x_c, y + jnp.concatenate(y_tiles, axis=1))
        return carry

    lax.fori_loop(0, nc, body, 0, unroll=16)


def _ssd_scan_kernel(x_ref, b_ref, c_ref, z_ref, dtf_ref, dtb_ref, biasf_ref, biasb_ref,
                     alogf_ref, alogb_ref, dskip_ref, yg_ref, ssq_ref,
                     y_acc, h_ref, r_cs, r_dt, t_cs, t_w, t_e, dec_ref):
    q = SSD_CHUNK
    decay = (r_cs, r_dt, t_cs, t_w, t_e, dec_ref)

    def emit_fwd(r0, x_c, y):
        y_acc[pl.ds(r0, q), :] = y + x_c * dskip_ref[...]

    def emit_bwd(r0, x_c, y):
        yg = (y_acc[pl.ds(r0, q), :] + y) * _silu(z_ref[0, pl.ds(r0, q), :].astype(F32))
        yg_ref[0, pl.ds(r0, q), :] = yg.astype(yg_ref.dtype)
        sq = yg * yg
        folded = sq[:, 0:V7X_LANES]
        for k in range(1, SSD_GROUP_DIM // V7X_LANES):
            folded = folded + sq[:, k * V7X_LANES:(k + 1) * V7X_LANES]
        ssq_ref[0, pl.ds(r0, q), :] = folded

    _ssd_prepare(False, dtf_ref, biasf_ref, alogf_ref, *decay)
    _ssd_direction(False, x_ref, b_ref, c_ref, h_ref, *decay, emit_fwd)
    _ssd_prepare(True, dtb_ref, biasb_ref, alogb_ref, *decay)
    _ssd_direction(True, x_ref, b_ref, c_ref, h_ref, *decay, emit_bwd)


def _ssd_scan(zxbc, dt_t, dt_bias, a_log, d_skip, *, batch, seq):
    g, hg, gd, n = SSD_GROUPS, SSD_HEADS_PER_GROUP, SSD_GROUP_DIM, SSD_STATE
    xcol = D_INNER // gd
    bcol = 2 * D_INNER // n
    ccol = (2 * D_INNER + SSD_BC_DIM) // n
    col = lambda v: v.reshape(2 * SSD_HEADS, 1)
    dskip_lanes = jnp.repeat(d_skip, SSD_HEAD_DIM).reshape(1, D_INNER)
    per_dir = lambda d: pl.BlockSpec((hg, 1), lambda b, j: (d * g + j, 0))
    rows = (seq // SSD_CHUNK) * hg
    vmem = (2 * (3 * _nbytes((seq, gd), F32) + 3 * _nbytes((seq, n), F32) + 2 * _nbytes((hg, seq), F32))
            + _nbytes((seq, gd), F32) + _nbytes((n, gd), F32) + 24 * _nbytes((SSD_CHUNK, gd), F32))
    return pl.pallas_call(
        _ssd_scan_kernel,
        grid=(batch, g),
        in_specs=[pl.BlockSpec((1, seq, gd), lambda b, j: (b, 0, xcol + j)),
                  pl.BlockSpec((1, seq, n), lambda b, j: (b, 0, bcol + j)),
                  pl.BlockSpec((1, seq, n), lambda b, j: (b, 0, ccol + j)),
                  pl.BlockSpec((1, seq, gd), lambda b, j: (b, 0, j)),
                  pl.BlockSpec((hg, seq), lambda b, j: (j, b)),
                  pl.BlockSpec((hg, seq), lambda b, j: (g + j, b)),
                  per_dir(0), per_dir(1), per_dir(0), per_dir(1),
                  pl.BlockSpec((1, gd), lambda b, j: (0, j))],
        out_specs=[pl.BlockSpec((1, seq, gd), lambda b, j: (b, 0, j)),
                   pl.BlockSpec((1, seq, V7X_LANES), lambda b, j: (b, 0, j))],
        out_shape=[jax.ShapeDtypeStruct((batch, seq, D_INNER), BF16),
                   jax.ShapeDtypeStruct((batch, seq, g * V7X_LANES), F32)],
        scratch_shapes=[pltpu.VMEM((seq, gd), F32), pltpu.VMEM((n, gd), F32)]
        + [pltpu.VMEM((rows, SSD_CHUNK), F32)] * 2 + [pltpu.VMEM((SSD_CHUNK, rows), F32)] * 3
        + [pltpu.VMEM((rows, gd), F32)],
        compiler_params=_params(("parallel", "parallel"), vmem),
        name="ssd_scan",
    )(zxbc, zxbc, zxbc, zxbc, dt_t, dt_t, col(dt_bias), col(dt_bias), col(a_log), col(a_log),
      dskip_lanes)


def _ssd_mixer(x, norm_g, w_in_bf, conv_w, conv_b, dt_bias, a_log, d_skip, ssd_norm, w_out, *,
               batch, seq):
    w_dt_t = w_in_bf[:, SSD_ZXBC_DIM:].T
    zxbc, (w_out_bf,) = _inproj_conv(x, norm_g, w_in_bf, conv_w, conv_b, [w_out], batch=batch,
                                     seq=seq, tn=512, mm_rows=512, conv_rows=128)
    dt_t = _rms_matmul_t(x, norm_g, w_dt_t, tm=1024)
    yg, ssq = _ssd_scan(zxbc, dt_t, dt_bias, a_log, d_skip, batch=batch, seq=seq)
    return _norm_matmul_res(yg.reshape(batch * seq, D_INNER), ssq.reshape(batch * seq, -1),
                            ssd_norm, w_out_bf, x, tm=512, tn=D_MODEL)


def _pool_kernel(u_ref, wg_ref, sc_ref, o_ref, pad_ref, *, rows):
    l, c = u_ref.shape[1], u_ref.shape[2]
    pad_ref[0:CONV_PAD_ROWS, :] = jnp.zeros((CONV_PAD_ROWS, c), F32)
    pad_ref[l + CONV_PAD_ROWS:l + 2 * CONV_PAD_ROWS, :] = jnp.zeros((CONV_PAD_ROWS, c), F32)
    pad_ref[CONV_PAD_ROWS:l + CONV_PAD_ROWS, :] = u_ref[0].astype(F32)

    for gi, win in enumerate(POOL_WINDOWS):
        half = win // 2

        @pl.when(pl.program_id(1) == gi)
        def _(half=half):
            def body(r, carry):
                r0 = pl.multiple_of(r * rows, rows)
                t = r0 + lax.broadcasted_iota(jnp.int32, (rows, 1), 0)
                cnt = jnp.minimum(t + half, l) - jnp.maximum(t - half, 0)
                inv = 1.0 / cnt.astype(F32)
                win = pad_ref[pl.ds(r0, rows + 2 * CONV_PAD_ROWS), :]
                s = win[CONV_PAD_ROWS - half:CONV_PAD_ROWS - half + rows, :]
                for k in range(1 - half, half):
                    s = s + win[CONV_PAD_ROWS + k:CONV_PAD_ROWS + k + rows, :]
                mix = (s * inv - win[CONV_PAD_ROWS:CONV_PAD_ROWS + rows, :]).astype(BF16)
                v = jnp.dot(mix, wg_ref[0], preferred_element_type=F32) * sc_ref[...]
                o_ref[0, pl.ds(r0, rows), :] = v.astype(o_ref.dtype)
                return carry

            lax.fori_loop(0, l // rows, body, 0)


def _pool(u, w_group, scale, *, batch, seq, rows):
    ng, gd = len(POOL_WINDOWS), POOL_GROUP_DIM
    vmem = (2 * (_nbytes((seq, gd), F32) + _nbytes((gd, gd), BF16) + _nbytes((seq, gd), BF16))
            + _nbytes((seq + 2 * CONV_PAD_ROWS, gd), F32) + 8 * _nbytes((rows, gd), F32))
    return pl.pallas_call(
        functools.partial(_pool_kernel, rows=rows),
        grid=(batch, ng),
        in_specs=[pl.BlockSpec((1, seq, gd), lambda b, j: (b, 0, j)),
                  pl.BlockSpec((1, gd, gd), lambda b, j: (j, 0, 0)),
                  pl.BlockSpec((1, gd), lambda b, j: (0, j))],
        out_specs=pl.BlockSpec((1, seq, gd), lambda b, j: (b, 0, j)),
        out_shape=jax.ShapeDtypeStruct((batch, seq, ng * gd), BF16),
        scratch_shapes=[pltpu.VMEM((seq + 2 * CONV_PAD_ROWS, gd), F32)],
        compiler_params=_params(("parallel", "parallel"), vmem),
        name="pool",
    )(u, w_group, scale.reshape(1, ng * gd))


def _pool_mixer(x, norm_g, w_in_bf, w_group_bf, scale, w_out_bf, *, batch, seq):
    u, _ = _rms_matmul(x, norm_g, w_in_bf, n=D_MODEL, tm=1024, tn=1024, out_dtype=F32)
    v = _pool(u.reshape(batch, seq, D_MODEL), w_group_bf, scale, batch=batch, seq=seq, rows=128)
    return _matmul_res(v.reshape(batch * seq, D_MODEL), w_out_bf, x, tm=1024, tn=512)


def _rmsnorm_kernel(x_ref, g_ref, o_ref):
    o_ref[...] = _rmsnorm_rows(x_ref[...], g_ref[...]).astype(o_ref.dtype)


def _rmsnorm(x, g, *, tm, out_dtype):
    t, d = x.shape
    return pl.pallas_call(
        _rmsnorm_kernel,
        grid=(t // tm,),
        in_specs=[pl.BlockSpec((tm, d), lambda i: (i, 0)), pl.BlockSpec((1, d), lambda i: (0, 0))],
        out_specs=pl.BlockSpec((tm, d), lambda i: (i, 0)),
        out_shape=jax.ShapeDtypeStruct((t, d), out_dtype),
        compiler_params=_params(("parallel",), 6 * _nbytes((tm, d), F32)),
        name="rmsnorm",
    )(x, g.reshape(1, d))


def kernel(x, ffn_norm, ffn_w_gate, ffn_w_up, ffn_w_down, mix_norm, ssd_w_in, ssd_conv_w,
           ssd_conv_b, ssd_dt_bias, ssd_a_log, ssd_d, ssd_norm, ssd_w_out, pool_w_in, pool_w_group,
           pool_scale, pool_w_out, final_norm):
    batch, seq, d = x.shape
    depth = ffn_norm.shape[0]
    h = x.reshape(batch * seq, d)

    n_ffn = 2 * depth
    ffn_stacks = (ffn_w_gate, ffn_w_up, ffn_w_down)
    ffn_w = [w[0, 0].astype(BF16) for w in ffn_stacks]
    pool_w_group2 = pool_w_group.reshape(pool_w_group.shape[0], -1, POOL_GROUP_DIM)

    def ffn(h, k, w_bf, mixer_raw):
        nxt = [(w, ((k + 1) // 2, (k + 1) % 2)) for w in ffn_stacks] if k + 1 < n_ffn else []
        h, cast = _ffn(h, ffn_norm[k // 2, k % 2], *w_bf, nxt + mixer_raw, tm=1024, tf=512)
        return h, cast[:len(nxt)], cast[len(nxt):]

    for i in range(depth):
        j = i // 2
        if i % 2 == 0:
            h, ffn_w, (w_in_bf,) = ffn(h, 2 * i, ffn_w, [(ssd_w_in, (j,))])
            h = _ssd_mixer(h, mix_norm[i], w_in_bf, ssd_conv_w[j], ssd_conv_b[j], ssd_dt_bias[j],
                           ssd_a_log[j], ssd_d[j], ssd_norm[j], (ssd_w_out, (j,)),
                           batch=batch, seq=seq)
        else:
            h, ffn_w, pool_w = ffn(h, 2 * i, ffn_w,
                                   [(pool_w_in, (j,)), (pool_w_group2, (j,)), (pool_w_out, (j,))])
            h = _pool_mixer(h, mix_norm[i], pool_w[0], pool_w[1].reshape(pool_w_group.shape[1:]),
                            pool_scale[j], pool_w[2], batch=batch, seq=seq)
        h, ffn_w, _ = ffn(h, 2 * i + 1, ffn_w, [])
    return _rmsnorm(h, final_norm, tm=512, out_dtype=F32).reshape(batch, seq, d)
```

```python
import functools

import jax
import jax.numpy as jnp
from jax import lax
from jax.experimental import pallas as pl
from jax.experimental.pallas import tpu as pltpu

F32 = jnp.float32
BF16 = jnp.bfloat16

EPS = 1e-6
D_MODEL = 2048
D_FF = 5632
D_INNER = 2 * D_MODEL
SSD_HEAD_DIM = 64
SSD_HEADS = D_INNER // SSD_HEAD_DIM
SSD_GROUPS = 8
SSD_HEADS_PER_GROUP = SSD_HEADS // SSD_GROUPS
SSD_GROUP_DIM = SSD_HEADS_PER_GROUP * SSD_HEAD_DIM
SSD_STATE = 128
SSD_CONV_W = 5
SSD_CHUNK = 128
SSD_BC_DIM = SSD_GROUPS * SSD_STATE
SSD_ZXBC_DIM = 2 * D_INNER + 2 * SSD_BC_DIM
POOL_WINDOWS = (2, 4, 8, 16)
POOL_GROUP_DIM = D_MODEL // len(POOL_WINDOWS)

V7X_VMEM_BYTES = 64 * 1024 * 1024
V7X_LANES = 128
V7X_SUBLANES = 8
BF16_SUBLANES = 2 * V7X_SUBLANES
V7X_MXU_WIDTH = 256
CONV_PAD_ROWS = V7X_SUBLANES
NORM_ROW_CHUNK = 128

def _params(semantics, vmem_bytes):
    return pltpu.CompilerParams(dimension_semantics=semantics,
                                vmem_limit_bytes=min(int(vmem_bytes), V7X_VMEM_BYTES))


def _nbytes(shape, dtype):
    n = 1
    for s in shape:
        n *= s
    return n * jnp.dtype(dtype).itemsize


def _rmsnorm_rows(x, g):
    ms = jnp.mean(x * x, axis=-1, keepdims=True)
    return x * lax.rsqrt(ms + EPS) * g


def _silu(v):
    return v * jax.nn.sigmoid(v)


def _cast_slabs(arrays, ni, nj):
    in_specs, out_specs, out_shapes, vmem = [], [], [], 0
    for stack, lead in arrays:
        r, c = stack.shape[-2:]
        tiles = [(br, bc) for br in range(BF16_SUBLANES, r + 1, BF16_SUBLANES) if r % br == 0
                 for bc in range(V7X_LANES, c + 1, V7X_LANES) if c % bc == 0
                 and (r // br) * (c // bc) <= ni * nj]
        br, bc = min(tiles, key=lambda t: t[0] * t[1])
        ncb, nb = c // bc, (r // br) * (c // bc)

        def tile_index(i, j, ncb=ncb, nb=nb):
            blk = jnp.minimum(i * nj + j, nb - 1)
            return blk // ncb, blk % ncb

        in_specs.append(pl.BlockSpec((None,) * len(lead) + (br, bc),
                                     lambda i, j, lead=lead, f=tile_index: lead + f(i, j)))
        out_specs.append(pl.BlockSpec((br, bc), tile_index))
        out_shapes.append(jax.ShapeDtypeStruct((r, c), BF16))
        vmem += 2 * (_nbytes((br, bc), F32) + _nbytes((br, bc), BF16))
    return in_specs, out_specs, out_shapes, vmem


def _run_casts(cast_in, cast_out):
    for src, dst in zip(cast_in, cast_out):
        dst[...] = src[...].astype(BF16)


def _rms_matmul_kernel(x_ref, g_ref, w_ref, *rest, n_cast):
    cast_in, o_ref = rest[:n_cast], rest[n_cast]
    cast_out, xn_ref = rest[n_cast + 1:2 * n_cast + 1], rest[2 * n_cast + 1]

    @pl.when(pl.program_id(1) == 0)
    def _():
        xn_ref[...] = _rmsnorm_rows(x_ref[...], g_ref[...]).astype(BF16)

    o_ref[...] = jnp.dot(xn_ref[...], w_ref[...], preferred_element_type=F32).astype(o_ref.dtype)
    _run_casts(cast_in, cast_out)


def _rms_matmul(x, g, w, cast_next=(), *, n, tm, tn, out_dtype):
    t, d = x.shape
    ni, nj = t // tm, n // tn
    cast_in, cast_out, cast_shapes, cast_vmem = _cast_slabs(cast_next, ni, nj)
    w_bufs, w_mode = _whole_weight(tn, w.shape[1])
    vmem = (2 * (_nbytes((tm, d), F32) + _nbytes((tm, tn), out_dtype))
            + w_bufs * _nbytes((d, tn), BF16)
            + _nbytes((tm, d), BF16) + _nbytes((tm, d), F32) + _nbytes((tm, tn), F32) + cast_vmem)
    outs = pl.pallas_call(
        functools.partial(_rms_matmul_kernel, n_cast=len(cast_next)),
        grid=(ni, nj),
        in_specs=[pl.BlockSpec((tm, d), lambda i, j: (i, 0)),
                  pl.BlockSpec((1, d), lambda i, j: (0, 0)),
                  pl.BlockSpec((d, tn), lambda i, j: (0, j), **w_mode)] + cast_in,
        out_specs=[pl.BlockSpec((tm, tn), lambda i, j: (i, j))] + cast_out,
        out_shape=[jax.ShapeDtypeStruct((t, n), out_dtype)] + cast_shapes,
        scratch_shapes=[pltpu.VMEM((tm, d), BF16)],
        compiler_params=_params(("parallel", "arbitrary"), vmem),
        name="rms_matmul",
    )(x, g.reshape(1, d), w, *[stack for stack, _ in cast_next])
    return outs[0], outs[1:]


def _rms_matmul_t_kernel(x_ref, g_ref, wt_ref, o_ref):
    xn = _rmsnorm_rows(x_ref[...], g_ref[...]).astype(BF16)
    o_ref[...] = lax.dot_general(wt_ref[...], xn, (((1,), (1,)), ((), ())),
                                 preferred_element_type=F32)


def _rms_matmul_t(x, g, wt, *, tm):
    t, d = x.shape
    n = wt.shape[0]
    vmem = (2 * (_nbytes((tm, d), F32) + _nbytes((n, d), BF16) + _nbytes((n, tm), F32))
            + 2 * _nbytes((tm, d), F32))
    return pl.pallas_call(
        _rms_matmul_t_kernel,
        grid=(t // tm,),
        in_specs=[pl.BlockSpec((tm, d), lambda i: (i, 0)),
                  pl.BlockSpec((1, d), lambda i: (0, 0)),
                  pl.BlockSpec((n, d), lambda i: (0, 0))],
        out_specs=pl.BlockSpec((n, tm), lambda i: (0, i)),
        out_shape=jax.ShapeDtypeStruct((n, t), F32),
        compiler_params=_params(("parallel",), vmem),
        name="rms_matmul_t",
    )(x, g.reshape(1, d), wt)


def _ffn_kernel(x_ref, g_ref, wg_ref, wu_ref, wd_ref, *rest, n_cast):
    cast_in, o_ref = rest[:n_cast], rest[n_cast]
    cast_out, xn_ref = rest[n_cast + 1:2 * n_cast + 1], rest[2 * n_cast + 1]

    @pl.when(pl.program_id(1) == 0)
    def _():
        x = x_ref[...]
        xn_ref[...] = _rmsnorm_rows(x, g_ref[...]).astype(BF16)
        o_ref[...] = x

    xn = xn_ref[...]
    gate = jnp.dot(xn, wg_ref[...], preferred_element_type=F32)
    up = jnp.dot(xn, wu_ref[...], preferred_element_type=F32)
    hidden = (_silu(gate) * up * 0.5).astype(BF16)
    o_ref[...] += jnp.dot(hidden, wd_ref[...], preferred_element_type=F32)
    _run_casts(cast_in, cast_out)


def _ffn(x, g, wg, wu, wd, cast_next, *, tm, tf):
    t, d = x.shape
    f = wg.shape[1]
    ni, nj = t // tm, f // tf
    cast_in, cast_out, cast_shapes, cast_vmem = _cast_slabs(cast_next, ni, nj)
    vmem = (2 * (2 * _nbytes((tm, d), F32) + 2 * _nbytes((d, tf), BF16) + _nbytes((tf, d), BF16))
            + _nbytes((tm, d), BF16) + 4 * _nbytes((tm, tf), F32) + cast_vmem)
    outs = pl.pallas_call(
        functools.partial(_ffn_kernel, n_cast=len(cast_next)),
        grid=(ni, nj),
        in_specs=[pl.BlockSpec((tm, d), lambda i, j: (i, 0)),
                  pl.BlockSpec((1, d), lambda i, j: (0, 0)),
                  pl.BlockSpec((d, tf), lambda i, j: (0, j)),
                  pl.BlockSpec((d, tf), lambda i, j: (0, j)),
                  pl.BlockSpec((tf, d), lambda i, j: (j, 0))] + cast_in,
        out_specs=[pl.BlockSpec((tm, d), lambda i, j: (i, 0))] + cast_out,
        out_shape=[jax.ShapeDtypeStruct((t, d), F32)] + cast_shapes,
        scratch_shapes=[pltpu.VMEM((tm, d), BF16)],
        compiler_params=_params(("parallel", "arbitrary"), vmem),
        name="ffn",
    )(x, g.reshape(1, d), wg, wu, wd, *[stack for stack, _ in cast_next])
    return outs[0], outs[1:]


def _matmul_res_kernel(a_ref, w_ref, r_ref, o_ref):
    o_ref[...] = r_ref[...] + jnp.dot(a_ref[...], w_ref[...], preferred_element_type=F32)


def _whole_weight(tn, n):
    if tn == n:
        return 1, dict(pipeline_mode=pl.Buffered(1))
    return 2, {}


def _matmul_res(a, w, res, *, tm, tn):
    t, k = a.shape
    n = w.shape[1]
    w_bufs, w_mode = _whole_weight(tn, n)
    vmem = (2 * (_nbytes((tm, k), a.dtype) + 2 * _nbytes((tm, tn), F32))
            + w_bufs * _nbytes((k, tn), BF16) + _nbytes((tm, tn), F32))
    return pl.pallas_call(
        _matmul_res_kernel,
        grid=(t // tm, n // tn),
        in_specs=[pl.BlockSpec((tm, k), lambda i, j: (i, 0)),
                  pl.BlockSpec((k, tn), lambda i, j: (0, j), **w_mode),
                  pl.BlockSpec((tm, tn), lambda i, j: (i, j))],
        out_specs=pl.BlockSpec((tm, tn), lambda i, j: (i, j)),
        out_shape=jax.ShapeDtypeStruct((t, n), F32),
        compiler_params=_params(("parallel", "arbitrary"), vmem),
        name="matmul_res",
    )(a, w, res)


def _norm_matmul_res_kernel(yg_ref, ssq_ref, g_ref, w_ref, r_ref, o_ref, an_ref):
    @pl.when(pl.program_id(1) == 0)
    def _():
        rc = NORM_ROW_CHUNK

        def body(r, carry):
            rows = pl.ds(pl.multiple_of(r * rc, rc), rc)
            ms = jnp.sum(ssq_ref[rows, :], axis=-1, keepdims=True) * (1.0 / yg_ref.shape[-1])
            an_ref[rows, :] = (yg_ref[rows, :].astype(F32) * lax.rsqrt(ms + EPS)
                               * g_ref[...]).astype(BF16)
            return carry

        lax.fori_loop(0, yg_ref.shape[0] // rc, body, 0)

    o_ref[...] = r_ref[...] + jnp.dot(an_ref[...], w_ref[...], preferred_element_type=F32)


def _norm_matmul_res(yg, ssq, g, w, res, *, tm, tn):
    t, k = yg.shape
    n = w.shape[1]
    s = ssq.shape[1]
    w_bufs, w_mode = _whole_weight(tn, n)
    vmem = (2 * (_nbytes((tm, k), yg.dtype) + _nbytes((tm, s), F32) + 2 * _nbytes((tm, tn), F32))
            + w_bufs * _nbytes((k, tn), BF16)
            + _nbytes((tm, k), BF16) + 2 * _nbytes((tm, tn), F32))
    return pl.pallas_call(
        _norm_matmul_res_kernel,
        grid=(t // tm, n // tn),
        in_specs=[pl.BlockSpec((tm, k), lambda i, j: (i, 0)),
                  pl.BlockSpec((tm, s), lambda i, j: (i, 0)),
                  pl.BlockSpec((1, k), lambda i, j: (0, 0)),
                  pl.BlockSpec((k, tn), lambda i, j: (0, j), **w_mode),
                  pl.BlockSpec((tm, tn), lambda i, j: (i, j))],
        out_specs=pl.BlockSpec((tm, tn), lambda i, j: (i, j)),
        out_shape=jax.ShapeDtypeStruct((t, n), F32),
        scratch_shapes=[pltpu.VMEM((tm, k), BF16)],
        compiler_params=_params(("parallel", "arbitrary"), vmem),
        name="norm_matmul_res",
    )(yg, ssq, g.reshape(1, k), w, res)


def _inproj_conv_kernel(x_ref, g_ref, w_ref, cw_ref, cb_ref, *rest, n_cast, z_tiles, mm_rows,
                        conv_rows):
    cast_in, o_ref = rest[:n_cast], rest[n_cast]
    cast_out = rest[n_cast + 1:2 * n_cast + 1]
    xn_ref, stage_ref = rest[2 * n_cast + 1:]
    l, tn = o_ref.shape[1], o_ref.shape[2]
    j = pl.program_id(1)
    half = SSD_CONV_W // 2

    @pl.when(j == 0)
    def _():
        rc = NORM_ROW_CHUNK

        def body(r, carry):
            rows = pl.ds(pl.multiple_of(r * rc, rc), rc)
            xn_ref[rows, :] = _rmsnorm_rows(x_ref[0, rows, :], g_ref[...]).astype(BF16)
            return carry

        lax.fori_loop(0, l // rc, body, 0)

    @pl.when(j < z_tiles)
    def _():
        o_ref[0] = jnp.dot(xn_ref[...], w_ref[...], preferred_element_type=F32).astype(o_ref.dtype)

    @pl.when(j >= z_tiles)
    def _():
        stage_ref[0:CONV_PAD_ROWS, :] = jnp.zeros((CONV_PAD_ROWS, tn), F32)
        stage_ref[l + CONV_PAD_ROWS:l + 2 * CONV_PAD_ROWS, :] = jnp.zeros((CONV_PAD_ROWS, tn), F32)
        n_mm = l // mm_rows
        for c in range(n_mm + 1):
            if c < n_mm:
                stage_ref[CONV_PAD_ROWS + c * mm_rows:CONV_PAD_ROWS + (c + 1) * mm_rows, :] = jnp.dot(
                    xn_ref[c * mm_rows:(c + 1) * mm_rows, :], w_ref[...],
                    preferred_element_type=F32)
            if c > 0:
                for r0 in range((c - 1) * mm_rows, c * mm_rows, conv_rows):
                    acc = jnp.broadcast_to(cb_ref[...], (conv_rows, tn))
                    for k in range(SSD_CONV_W):
                        lo = r0 + CONV_PAD_ROWS + k - half
                        acc = acc + stage_ref[lo:lo + conv_rows, :] * cw_ref[k:k + 1, :]
                    o_ref[0, r0:r0 + conv_rows, :] = _silu(acc).astype(o_ref.dtype)

    _run_casts(cast_in, cast_out)


def _inproj_conv(x, g, w, conv_w, conv_b, cast_next, *, batch, seq, tn, mm_rows, conv_rows):
    d = x.shape[1]
    z_tiles, nj = D_INNER // tn, SSD_ZXBC_DIM // tn
    cdim = conv_w.shape[1]
    conv_tile = lambda b, j: (0, jnp.maximum(j - z_tiles, 0))
    cast_in, cast_out, cast_shapes, cast_vmem = _cast_slabs(cast_next, batch, nj)
    stage_shape = (seq + 2 * CONV_PAD_ROWS, tn)
    vmem = (2 * (_nbytes((seq, d), F32) + _nbytes((d, tn), BF16) + _nbytes((seq, tn), BF16))
            + _nbytes((seq, d), BF16) + _nbytes(stage_shape, F32) + 6 * _nbytes((mm_rows, tn), F32)
            + cast_vmem)
    outs = pl.pallas_call(
        functools.partial(_inproj_conv_kernel, n_cast=len(cast_next), z_tiles=z_tiles,
                          mm_rows=mm_rows, conv_rows=conv_rows),
        grid=(batch, nj),
        in_specs=[pl.BlockSpec((1, seq, d), lambda b, j: (b, 0, 0)),
                  pl.BlockSpec((1, d), lambda b, j: (0, 0)),
                  pl.BlockSpec((d, tn), lambda b, j: (0, j)),
                  pl.BlockSpec((SSD_CONV_W, tn), conv_tile),
                  pl.BlockSpec((1, tn), conv_tile)] + cast_in,
        out_specs=[pl.BlockSpec((1, seq, tn), lambda b, j: (b, 0, j))] + cast_out,
        out_shape=[jax.ShapeDtypeStruct((batch, seq, SSD_ZXBC_DIM), BF16)] + cast_shapes,
        scratch_shapes=[pltpu.VMEM((seq, d), BF16), pltpu.VMEM(stage_shape, F32)],
        compiler_params=_params(("parallel", "arbitrary"), vmem),
        name="inproj_conv",
    )(x.reshape(batch, seq, d), g.reshape(1, d), w, conv_w, conv_b.reshape(1, cdim),
      *[stack for stack, _ in cast_next])
    return outs[0], outs[1:]


def _split_dot(v, m01):
    hi = v.astype(BF16)
    r1 = v - hi.astype(F32)
    mid = r1.astype(BF16)
    lo = (r1 - mid.astype(F32)).astype(BF16)
    out = jnp.dot(hi, m01, preferred_element_type=F32)
    out = out + jnp.dot(mid, m01, preferred_element_type=F32)
    return out + jnp.dot(lo, m01, preferred_element_type=F32)


def _softplus(v):
    return jnp.maximum(v, 0.0) + jnp.log1p(jnp.exp(-jnp.abs(v)))


def _log2(n):
    assert n & (n - 1) == 0
    return n.bit_length() - 1


def _ssd_prepare(reverse, dt_ref, bias_ref, alog_ref, r_cs, r_dt, t_cs, t_w, t_e, dec_ref):
    q, hg, p = SSD_CHUNK, SSD_HEADS_PER_GROUP, SSD_HEAD_DIM
    nc = dt_ref.shape[1] // q
    rows = nc * hg
    dt_all = _softplus(dt_ref[...] + bias_ref[...])
    dta_all = dt_all * (-jnp.exp(alog_ref[...]))
    stack = lambda v: jnp.concatenate([v[:, c * q:(c + 1) * q] for c in range(nc)], axis=0)
    dt_r = stack(dt_all)
    src = lax.broadcasted_iota(jnp.int32, (q, q), 0)
    dst = lax.broadcasted_iota(jnp.int32, (q, q), 1)
    cum01 = ((src >= dst) if reverse else (src <= dst)).astype(BF16)
    last_pos = 0 if reverse else q - 1
    cs = _split_dot(stack(dta_all), cum01)
    cs_last = cs[:, last_pos:last_pos + 1]
    cs_t = cs.T
    r_cs[...] = cs
    r_dt[...] = dt_r
    t_cs[...] = cs_t
    t_w[...] = (dt_r * jnp.exp(cs_last - cs)).T
    t_e[...] = jnp.exp(cs).T
    band_row = lax.shift_right_logical(lax.broadcasted_iota(jnp.int32, (rows, rows), 0), _log2(hg))
    band_col = lax.shift_right_logical(lax.broadcasted_iota(jnp.int32, (rows, rows), 1), _log2(hg))
    dec_sel = jnp.where(band_row == band_col, jnp.exp(cs_t[last_pos:last_pos + 1, :]), 0.0)
    head_of_row = lax.broadcasted_iota(jnp.int32, (rows, SSD_GROUP_DIM), 0) & (hg - 1)
    head_of_lane = lax.shift_right_logical(
        lax.broadcasted_iota(jnp.int32, (rows, SSD_GROUP_DIM), 1), _log2(p))
    dec_ref[...] = _split_dot(dec_sel, (head_of_row == head_of_lane).astype(BF16))


def _ssd_direction(reverse, x_ref, b_ref, c_ref, h_ref, r_cs, r_dt, t_cs, t_w, t_e, dec_ref, emit):
    q, hg, p, gd = SSD_CHUNK, SSD_HEADS_PER_GROUP, SSD_HEAD_DIM, SSD_GROUP_DIM
    nc = x_ref.shape[1] // q
    rows = nc * hg
    per_tile = V7X_MXU_WIDTH // p
    tgt = lax.broadcasted_iota(jnp.int32, (q, q), 0)
    src = lax.broadcasted_iota(jnp.int32, (q, q), 1)
    valid = (tgt <= src) if reverse else (tgt >= src)
    assert rows == V7X_LANES and 2 * p == V7X_LANES
    lane = lax.broadcasted_iota(jnp.int32, (q, V7X_LANES), 1)
    second_of_pair = lax.shift_right_logical(lane, _log2(p))
    tile_head = lax.shift_right_logical(
        lax.broadcasted_iota(jnp.int32, (q, V7X_MXU_WIDTH), 1), _log2(p))
    h_ref[...] = jnp.zeros(h_ref.shape, F32)

    def body(step, carry):
        chunk = (nc - 1 - step) if reverse else step
        r0 = pl.multiple_of(chunk * q, q)
        band = pl.ds(pl.multiple_of(chunk * hg, hg), hg)
        lane0 = chunk * hg

        def spread(t_ref):
            cols = t_ref[...]
            return jnp.concatenate(
                [jnp.take_along_axis(cols, lane0 + k + second_of_pair, axis=1)
                 for k in range(0, hg, 2)], axis=1)

        cs_all = t_cs[...]
        cs_rows = r_cs[band, :]
        dt_rows = r_dt[band, :]
        dec = dec_ref[band, :][0:1, :]

        x_bf = x_ref[0, pl.ds(r0, q), :]
        x_c = x_bf.astype(F32)
        b_bf = b_ref[0, pl.ds(r0, q), :]
        c_bf = c_ref[0, pl.ds(r0, q), :]
        cb = lax.dot_general(c_bf, b_bf, (((1,), (1,)), ((), ())),
                             preferred_element_type=F32)
        h_in = h_ref[...]
        y = jnp.dot(c_bf, h_in.astype(BF16), preferred_element_type=F32) * spread(t_e)
        wx = (x_c * spread(t_w)).astype(BF16)
        h_ref[...] = h_in * dec + lax.dot_general(b_bf, wx, (((0,), (0,)), ((), ())),
                                                  preferred_element_type=F32)

        y_tiles = []
        for t0 in range(0, hg, per_tile):
            ms, xs = [], []
            x_tile = x_bf[:, t0 * p:(t0 + per_tile) * p]
            for k in range(per_tile):
                h = t0 + k
                cs_tgt = jnp.take_along_axis(cs_all, jnp.broadcast_to(lane0 + h, lane.shape), axis=1)
                seg = cs_tgt - cs_rows[h:h + 1, :]
                lmat = jnp.exp(jnp.where(valid, seg, -jnp.inf))
                ms.append((cb * lmat * dt_rows[h:h + 1, :]).astype(BF16))
                xs.append(jnp.where(tile_head == k, x_tile, jnp.zeros_like(x_tile)))
            y_tiles.append(jnp.dot(jnp.concatenate(ms, axis=1), jnp.concatenate(xs, axis=0),
                                   preferred_element_type=F32))
        emit(r0, x_c, y + jnp.concatenate(y_tiles, axis=1))
        return carry

    lax.fori_loop(0, nc, body, 0, unroll=16)


def _ssd_scan_kernel(x_ref, b_ref, c_ref, z_ref, dtf_ref, dtb_ref, biasf_ref, biasb_ref,
                     alogf_ref, alogb_ref, dskip_ref, yg_ref, ssq_ref,
                     y_acc, h_ref, r_cs, r_dt, t_cs, t_w, t_e, dec_ref):
    q = SSD_CHUNK
    decay = (r_cs, r_dt, t_cs, t_w, t_e, dec_ref)

    def emit_fwd(r0, x_c, y):
        y_acc[pl.ds(r0, q), :] = y + x_c * dskip_ref[...]

    def emit_bwd(r0, x_c, y):
        yg = (y_acc[pl.ds(r0, q), :] + y) * _silu(z_ref[0, pl.ds(r0, q), :].astype(F32))
        yg_ref[0, pl.ds(r0, q), :] = yg.astype(yg_ref.dtype)
        sq = yg * yg
        folded = sq[:, 0:V7X_LANES]
        for k in range(1, SSD_GROUP_DIM // V7X_LANES):
            folded = folded + sq[:, k * V7X_LANES:(k + 1) * V7X_LANES]
        ssq_ref[0, pl.ds(r0, q), :] = folded

    _ssd_prepare(False, dtf_ref, biasf_ref, alogf_ref, *decay)
    _ssd_direction(False, x_ref, b_ref, c_ref, h_ref, *decay, emit_fwd)
    _ssd_prepare(True, dtb_ref, biasb_ref, alogb_ref, *decay)
    _ssd_direction(True, x_ref, b_ref, c_ref, h_ref, *decay, emit_bwd)


def _ssd_scan(zxbc, dt_t, dt_bias, a_log, d_skip, *, batch, seq):
    g, hg, gd, n = SSD_GROUPS, SSD_HEADS_PER_GROUP, SSD_GROUP_DIM, SSD_STATE
    xcol = D_INNER // gd
    bcol = 2 * D_INNER // n
    ccol = (2 * D_INNER + SSD_BC_DIM) // n
    col = lambda v: v.reshape(2 * SSD_HEADS, 1)
    dskip_lanes = jnp.repeat(d_skip, SSD_HEAD_DIM).reshape(1, D_INNER)
    per_dir = lambda d: pl.BlockSpec((hg, 1), lambda b, j: (d * g + j, 0))
    rows = (seq // SSD_CHUNK) * hg
    vmem = (2 * (3 * _nbytes((seq, gd), F32) + 3 * _nbytes((seq, n), F32) + 2 * _nbytes((hg, seq), F32))
            + _nbytes((seq, gd), F32) + _nbytes((n, gd), F32) + 24 * _nbytes((SSD_CHUNK, gd), F32))
    return pl.pallas_call(
        _ssd_scan_kernel,
        grid=(batch, g),
        in_specs=[pl.BlockSpec((1, seq, gd), lambda b, j: (b, 0, xcol + j)),
                  pl.BlockSpec((1, seq, n), lambda b, j: (b, 0, bcol + j)),
                  pl.BlockSpec((1, seq, n), lambda b, j: (b, 0, ccol + j)),
                  pl.BlockSpec((1, seq, gd), lambda b, j: (b, 0, j)),
                  pl.BlockSpec((hg, seq), lambda b, j: (j, b)),
                  pl.BlockSpec((hg, seq), lambda b, j: (g + j, b)),
                  per_dir(0), per_dir(1), per_dir(0), per_dir(1),
                  pl.BlockSpec((1, gd), lambda b, j: (0, j))],
        out_specs=[pl.BlockSpec((1, seq, gd), lambda b, j: (b, 0, j)),
                   pl.BlockSpec((1, seq, V7X_LANES), lambda b, j: (b, 0, j))],
        out_shape=[jax.ShapeDtypeStruct((batch, seq, D_INNER), BF16),
                   jax.ShapeDtypeStruct((batch, seq, g * V7X_LANES), F32)],
        scratch_shapes=[pltpu.VMEM((seq, gd), F32), pltpu.VMEM((n, gd), F32)]
        + [pltpu.VMEM((rows, SSD_CHUNK), F32)] * 2 + [pltpu.VMEM((SSD_CHUNK, rows), F32)] * 3
        + [pltpu.VMEM((rows, gd), F32)],
        compiler_params=_params(("parallel", "parallel"), vmem),
        name="ssd_scan",
    )(zxbc, zxbc, zxbc, zxbc, dt_t, dt_t, col(dt_bias), col(dt_bias), col(a_log), col(a_log),
      dskip_lanes)


def _ssd_mixer(x, norm_g, w_in_bf, conv_w, conv_b, dt_bias, a_log, d_skip, ssd_norm, w_out, *,
               batch, seq):
    w_dt_t = w_in_bf[:, SSD_ZXBC_DIM:].T
    zxbc, (w_out_bf,) = _inproj_conv(x, norm_g, w_in_bf, conv_w, conv_b, [w_out], batch=batch,
                                     seq=seq, tn=512, mm_rows=512, conv_rows=128)
    dt_t = _rms_matmul_t(x, norm_g, w_dt_t, tm=1024)
    yg, ssq = _ssd_scan(zxbc, dt_t, dt_bias, a_log, d_skip, batch=batch, seq=seq)
    return _norm_matmul_res(yg.reshape(batch * seq, D_INNER), ssq.reshape(batch * seq, -1),
                            ssd_norm, w_out_bf, x, tm=512, tn=D_MODEL)


def _pool_kernel(u_ref, wg_ref, sc_ref, o_ref, pad_ref, *, rows):
    l, c = u_ref.shape[1], u_ref.shape[2]
    pad_ref[0:CONV_PAD_ROWS, :] = jnp.zeros((CONV_PAD_ROWS, c), F32)
    pad_ref[l + CONV_PAD_ROWS:l + 2 * CONV_PAD_ROWS, :] = jnp.zeros((CONV_PAD_ROWS, c), F32)
    pad_ref[CONV_PAD_ROWS:l + CONV_PAD_ROWS, :] = u_ref[0].astype(F32)

    for gi, win in enumerate(POOL_WINDOWS):
        half = win // 2

        @pl.when(pl.program_id(1) == gi)
        def _(half=half):
            def body(r, carry):
                r0 = pl.multiple_of(r * rows, rows)
                t = r0 + lax.broadcasted_iota(jnp.int32, (rows, 1), 0)
                cnt = jnp.minimum(t + half, l) - jnp.maximum(t - half, 0)
                inv = 1.0 / cnt.astype(F32)
                win = pad_ref[pl.ds(r0, rows + 2 * CONV_PAD_ROWS), :]
                s = win[CONV_PAD_ROWS - half:CONV_PAD_ROWS - half + rows, :]
                for k in range(1 - half, half):
                    s = s + win[CONV_PAD_ROWS + k:CONV_PAD_ROWS + k + rows, :]
                mix = (s * inv - win[CONV_PAD_ROWS:CONV_PAD_ROWS + rows, :]).astype(BF16)
                v = jnp.dot(mix, wg_ref[0], preferred_element_type=F32) * sc_ref[...]
                o_ref[0, pl.ds(r0, rows), :] = v.astype(o_ref.dtype)
                return carry

            lax.fori_loop(0, l // rows, body, 0)


def _pool(u, w_group, scale, *, batch, seq, rows):
    ng, gd = len(POOL_WINDOWS), POOL_GROUP_DIM
    vmem = (2 * (_nbytes((seq, gd), F32) + _nbytes((gd, gd), BF16) + _nbytes((seq, gd), BF16))
            + _nbytes((seq + 2 * CONV_PAD_ROWS, gd), F32) + 8 * _nbytes((rows, gd), F32))
    return pl.pallas_call(
        functools.partial(_pool_kernel, rows=rows),
        grid=(batch, ng),
        in_specs=[pl.BlockSpec((1, seq, gd), lambda b, j: (b, 0, j)),
                  pl.BlockSpec((1, gd, gd), lambda b, j: (j, 0, 0)),
                  pl.BlockSpec((1, gd), lambda b, j: (0, j))],
        out_specs=pl.BlockSpec((1, seq, gd), lambda b, j: (b, 0, j)),
        out_shape=jax.ShapeDtypeStruct((batch, seq, ng * gd), BF16),
        scratch_shapes=[pltpu.VMEM((seq + 2 * CONV_PAD_ROWS, gd), F32)],
        compiler_params=_params(("parallel", "parallel"), vmem),
        name="pool",
    )(u, w_group, scale.reshape(1, ng * gd))


def _pool_mixer(x, norm_g, w_in_bf, w_group_bf, scale, w_out_bf, *, batch, seq):
    u, _ = _rms_matmul(x, norm_g, w_in_bf, n=D_MODEL, tm=512, tn=D_MODEL, out_dtype=BF16)
    v = _pool(u.reshape(batch, seq, D_MODEL), w_group_bf, scale, batch=batch, seq=seq, rows=128)
    return _matmul_res(v.reshape(batch * seq, D_MODEL), w_out_bf, x, tm=512, tn=D_MODEL)


def _rmsnorm_kernel(x_ref, g_ref, o_ref):
    o_ref[...] = _rmsnorm_rows(x_ref[...], g_ref[...]).astype(o_ref.dtype)


def _rmsnorm(x, g, *, tm, out_dtype):
    t, d = x.shape
    return pl.pallas_call(
        _rmsnorm_kernel,
        grid=(t // tm,),
        in_specs=[pl.BlockSpec((tm, d), lambda i: (i, 0)), pl.BlockSpec((1, d), lambda i: (0, 0))],
        out_specs=pl.BlockSpec((tm, d), lambda i: (i, 0)),
        out_shape=jax.ShapeDtypeStruct((t, d), out_dtype),
        compiler_params=_params(("parallel",), 6 * _nbytes((tm, d), F32)),
        name="rmsnorm",
    )(x, g.reshape(1, d))


def kernel(x, ffn_norm, ffn_w_gate, ffn_w_up, ffn_w_down, mix_norm, ssd_w_in, ssd_conv_w,
           ssd_conv_b, ssd_dt_bias, ssd_a_log, ssd_d, ssd_norm, ssd_w_out, pool_w_in, pool_w_group,
           pool_scale, pool_w_out, final_norm):
    batch, seq, d = x.shape
    depth = ffn_norm.shape[0]
    h = x.reshape(batch * seq, d)

    n_ffn = 2 * depth
    ffn_stacks = (ffn_w_gate, ffn_w_up, ffn_w_down)
    ffn_w = [w[0, 0].astype(BF16) for w in ffn_stacks]
    pool_w_group2 = pool_w_group.reshape(pool_w_group.shape[0], -1, POOL_GROUP_DIM)

    def ffn(h, k, w_bf, mixer_raw):
        nxt = [(w, ((k + 1) // 2, (k + 1) % 2)) for w in ffn_stacks] if k + 1 < n_ffn else []
        h, cast = _ffn(h, ffn_norm[k // 2, k % 2], *w_bf, nxt + mixer_raw, tm=1024, tf=512)
        return h, cast[:len(nxt)], cast[len(nxt):]

    for i in range(depth):
        j = i // 2
        if i % 2 == 0:
            h, ffn_w, (w_in_bf,) = ffn(h, 2 * i, ffn_w, [(ssd_w_in, (j,))])
            h = _ssd_mixer(h, mix_norm[i], w_in_bf, ssd_conv_w[j], ssd_conv_b[j], ssd_dt_bias[j],
                           ssd_a_log[j], ssd_d[j], ssd_norm[j], (ssd_w_out, (j,)),
                           batch=batch, seq=seq)
        else:
            h, ffn_w, pool_w = ffn(h, 2 * i, ffn_w,
                                   [(pool_w_in, (j,)), (pool_w_group2, (j,)), (pool_w_out, (j,))])
            h = _pool_mixer(h, mix_norm[i], pool_w[0], pool_w[1].reshape(pool_w_group.shape[1:]),
                            pool_scale[j], pool_w[2], batch=batch, seq=seq)
        h, ffn_w, _ = ffn(h, 2 * i + 1, ffn_w, [])
    return _rmsnorm(h, final_norm, tm=512, out_dtype=F32).reshape(batch, seq, d)
```

```python
import functools

import jax
import jax.numpy as jnp
from jax import lax
from jax.experimental import pallas as pl
from jax.experimental.pallas import tpu as pltpu

F32 = jnp.float32
BF16 = jnp.bfloat16

EPS = 1e-6
D_MODEL = 2048
D_FF = 5632
D_INNER = 2 * D_MODEL
SSD_HEAD_DIM = 64
SSD_HEADS = D_INNER // SSD_HEAD_DIM
SSD_GROUPS = 8
SSD_HEADS_PER_GROUP = SSD_HEADS // SSD_GROUPS
SSD_GROUP_DIM = SSD_HEADS_PER_GROUP * SSD_HEAD_DIM
SSD_STATE = 128
SSD_CONV_W = 5
SSD_CHUNK = 128
SSD_BC_DIM = SSD_GROUPS * SSD_STATE
SSD_ZXBC_DIM = 2 * D_INNER + 2 * SSD_BC_DIM
POOL_WINDOWS = (2, 4, 8, 16)
POOL_GROUP_DIM = D_MODEL // len(POOL_WINDOWS)

V7X_VMEM_BYTES = 64 * 1024 * 1024
V7X_LANES = 128
V7X_SUBLANES = 8
BF16_SUBLANES = 2 * V7X_SUBLANES
V7X_MXU_WIDTH = 256
CONV_PAD_ROWS = V7X_SUBLANES
NORM_ROW_CHUNK = 128

def _params(semantics, vmem_bytes):
    return pltpu.CompilerParams(dimension_semantics=semantics,
                                vmem_limit_bytes=min(int(vmem_bytes), V7X_VMEM_BYTES))


def _nbytes(shape, dtype):
    n = 1
    for s in shape:
        n *= s
    return n * jnp.dtype(dtype).itemsize


def _rmsnorm_rows(x, g):
    ms = jnp.mean(x * x, axis=-1, keepdims=True)
    return x * lax.rsqrt(ms + EPS) * g


def _silu(v):
    return v * jax.nn.sigmoid(v)


def _cast_slabs(arrays, ni, nj):
    in_specs, out_specs, out_shapes, vmem = [], [], [], 0
    for stack, lead in arrays:
        r, c = stack.shape[-2:]
        tiles = [(br, bc) for br in range(BF16_SUBLANES, r + 1, BF16_SUBLANES) if r % br == 0
                 for bc in range(V7X_LANES, c + 1, V7X_LANES) if c % bc == 0
                 and (r // br) * (c // bc) <= ni * nj]
        br, bc = min(tiles, key=lambda t: t[0] * t[1])
        ncb, nb = c // bc, (r // br) * (c // bc)

        def tile_index(i, j, ncb=ncb, nb=nb):
            blk = jnp.minimum(i * nj + j, nb - 1)
            return blk // ncb, blk % ncb

        in_specs.append(pl.BlockSpec((None,) * len(lead) + (br, bc),
                                     lambda i, j, lead=lead, f=tile_index: lead + f(i, j)))
        out_specs.append(pl.BlockSpec((br, bc), tile_index))
        out_shapes.append(jax.ShapeDtypeStruct((r, c), BF16))
        vmem += 2 * (_nbytes((br, bc), F32) + _nbytes((br, bc), BF16))
    return in_specs, out_specs, out_shapes, vmem


def _run_casts(cast_in, cast_out):
    for src, dst in zip(cast_in, cast_out):
        dst[...] = src[...].astype(BF16)


def _rms_matmul_kernel(x_ref, g_ref, w_ref, *rest, n_cast):
    cast_in, o_ref = rest[:n_cast], rest[n_cast]
    cast_out, xn_ref = rest[n_cast + 1:2 * n_cast + 1], rest[2 * n_cast + 1]

    @pl.when(pl.program_id(1) == 0)
    def _():
        xn_ref[...] = _rmsnorm_rows(x_ref[...], g_ref[...]).astype(BF16)

    o_ref[...] = jnp.dot(xn_ref[...], w_ref[...], preferred_element_type=F32).astype(o_ref.dtype)
    _run_casts(cast_in, cast_out)


def _rms_matmul(x, g, w, cast_next=(), *, n, tm, tn, out_dtype):
    t, d = x.shape
    ni, nj = t // tm, n // tn
    cast_in, cast_out, cast_shapes, cast_vmem = _cast_slabs(cast_next, ni, nj)
    w_bufs, w_mode = _whole_weight(tn, w.shape[1])
    vmem = (2 * (_nbytes((tm, d), F32) + _nbytes((tm, tn), out_dtype))
            + w_bufs * _nbytes((d, tn), BF16)
            + _nbytes((tm, d), BF16) + _nbytes((tm, d), F32) + _nbytes((tm, tn), F32) + cast_vmem)
    outs = pl.pallas_call(
        functools.partial(_rms_matmul_kernel, n_cast=len(cast_next)),
        grid=(ni, nj),
        in_specs=[pl.BlockSpec((tm, d), lambda i, j: (i, 0)),
                  pl.BlockSpec((1, d), lambda i, j: (0, 0)),
                  pl.BlockSpec((d, tn), lambda i, j: (0, j), **w_mode)] + cast_in,
        out_specs=[pl.BlockSpec((tm, tn), lambda i, j: (i, j))] + cast_out,
        out_shape=[jax.ShapeDtypeStruct((t, n), out_dtype)] + cast_shapes,
        scratch_shapes=[pltpu.VMEM((tm, d), BF16)],
        compiler_params=_params(("parallel", "arbitrary"), vmem),
        name="rms_matmul",
    )(x, g.reshape(1, d), w, *[stack for stack, _ in cast_next])
    return outs[0], outs[1:]


def _rms_matmul_t_kernel(x_ref, g_ref, wt_ref, o_ref):
    xn = _rmsnorm_rows(x_ref[...], g_ref[...]).astype(BF16)
    o_ref[...] = lax.dot_general(wt_ref[...], xn, (((1,), (1,)), ((), ())),
                                 preferred_element_type=F32)


def _rms_matmul_t(x, g, wt, *, tm):
    t, d = x.shape
    n = wt.shape[0]
    vmem = (2 * (_nbytes((tm, d), F32) + _nbytes((n, d), BF16) + _nbytes((n, tm), F32))
            + 2 * _nbytes((tm, d), F32))
    return pl.pallas_call(
        _rms_matmul_t_kernel,
        grid=(t // tm,),
        in_specs=[pl.BlockSpec((tm, d), lambda i: (i, 0)),
                  pl.BlockSpec((1, d), lambda i: (0, 0)),
                  pl.BlockSpec((n, d), lambda i: (0, 0))],
        out_specs=pl.BlockSpec((n, tm), lambda i: (0, i)),
        out_shape=jax.ShapeDtypeStruct((n, t), F32),
        compiler_params=_params(("parallel",), vmem),
        name="rms_matmul_t",
    )(x, g.reshape(1, d), wt)


def _ffn_kernel(x_ref, g_ref, wg_ref, wu_ref, wd_ref, gf_ref, *rest, n_cast, final_norm):
    cast_in, o_ref = rest[:n_cast], rest[n_cast]
    cast_out, xn_ref = rest[n_cast + 1:2 * n_cast + 1], rest[2 * n_cast + 1]

    @pl.when(pl.program_id(1) == 0)
    def _():
        x = x_ref[...]
        xn_ref[...] = _rmsnorm_rows(x, g_ref[...]).astype(BF16)
        o_ref[...] = x

    xn = xn_ref[...]
    gate = jnp.dot(xn, wg_ref[...], preferred_element_type=F32)
    up = jnp.dot(xn, wu_ref[...], preferred_element_type=F32)
    hidden = (_silu(gate) * up * 0.5).astype(BF16)
    o_ref[...] += jnp.dot(hidden, wd_ref[...], preferred_element_type=F32)
    _run_casts(cast_in, cast_out)

    if final_norm:
        @pl.when(pl.program_id(1) == pl.num_programs(1) - 1)
        def _():
            rc = NORM_ROW_CHUNK

            def body(r, carry):
                rows = pl.ds(pl.multiple_of(r * rc, rc), rc)
                o_ref[rows, :] = _rmsnorm_rows(o_ref[rows, :], gf_ref[...])
                return carry

            lax.fori_loop(0, o_ref.shape[0] // rc, body, 0)


def _ffn(x, g, wg, wu, wd, cast_next, *, tm, tf, final_g=None):
    t, d = x.shape
    gf = g if final_g is None else final_g
    f = wg.shape[1]
    ni, nj = t // tm, f // tf
    cast_in, cast_out, cast_shapes, cast_vmem = _cast_slabs(cast_next, ni, nj)
    vmem = (2 * (2 * _nbytes((tm, d), F32) + 2 * _nbytes((d, tf), BF16) + _nbytes((tf, d), BF16))
            + _nbytes((tm, d), BF16) + 4 * _nbytes((tm, tf), F32) + cast_vmem)
    outs = pl.pallas_call(
        functools.partial(_ffn_kernel, n_cast=len(cast_next), final_norm=final_g is not None),
        grid=(ni, nj),
        in_specs=[pl.BlockSpec((tm, d), lambda i, j: (i, 0)),
                  pl.BlockSpec((1, d), lambda i, j: (0, 0)),
                  pl.BlockSpec((d, tf), lambda i, j: (0, j)),
                  pl.BlockSpec((d, tf), lambda i, j: (0, j)),
                  pl.BlockSpec((tf, d), lambda i, j: (j, 0)),
                  pl.BlockSpec((1, d), lambda i, j: (0, 0))] + cast_in,
        out_specs=[pl.BlockSpec((tm, d), lambda i, j: (i, 0))] + cast_out,
        out_shape=[jax.ShapeDtypeStruct((t, d), F32)] + cast_shapes,
        scratch_shapes=[pltpu.VMEM((tm, d), BF16)],
        compiler_params=_params(("parallel", "arbitrary"), vmem),
        name="ffn",
    )(x, g.reshape(1, d), wg, wu, wd, gf.reshape(1, d), *[stack for stack, _ in cast_next])
    return outs[0], outs[1:]


def _matmul_res_kernel(a_ref, w_ref, r_ref, o_ref):
    o_ref[...] = r_ref[...] + jnp.dot(a_ref[...], w_ref[...], preferred_element_type=F32)


def _whole_weight(tn, n):
    if tn == n:
        return 1, dict(pipeline_mode=pl.Buffered(1))
    return 2, {}


def _matmul_res(a, w, res, *, tm, tn):
    t, k = a.shape
    n = w.shape[1]
    w_bufs, w_mode = _whole_weight(tn, n)
    vmem = (2 * (_nbytes((tm, k), a.dtype) + 2 * _nbytes((tm, tn), F32))
            + w_bufs * _nbytes((k, tn), BF16) + _nbytes((tm, tn), F32))
    return pl.pallas_call(
        _matmul_res_kernel,
        grid=(t // tm, n // tn),
        in_specs=[pl.BlockSpec((tm, k), lambda i, j: (i, 0)),
                  pl.BlockSpec((k, tn), lambda i, j: (0, j), **w_mode),
                  pl.BlockSpec((tm, tn), lambda i, j: (i, j))],
        out_specs=pl.BlockSpec((tm, tn), lambda i, j: (i, j)),
        out_shape=jax.ShapeDtypeStruct((t, n), F32),
        compiler_params=_params(("parallel", "arbitrary"), vmem),
        name="matmul_res",
    )(a, w, res)


def _norm_matmul_res_kernel(yg_ref, ssq_ref, g_ref, w_ref, r_ref, o_ref, an_ref):
    @pl.when(pl.program_id(1) == 0)
    def _():
        rc = NORM_ROW_CHUNK

        def body(r, carry):
            rows = pl.ds(pl.multiple_of(r * rc, rc), rc)
            ms = jnp.sum(ssq_ref[rows, :], axis=-1, keepdims=True) * (1.0 / yg_ref.shape[-1])
            an_ref[rows, :] = (yg_ref[rows, :].astype(F32) * lax.rsqrt(ms + EPS)
                               * g_ref[...]).astype(BF16)
            return carry

        lax.fori_loop(0, yg_ref.shape[0] // rc, body, 0)

    o_ref[...] = r_ref[...] + jnp.dot(an_ref[...], w_ref[...], preferred_element_type=F32)


def _norm_matmul_res(yg, ssq, g, w, res, *, tm, tn):
    t, k = yg.shape
    n = w.shape[1]
    s = ssq.shape[1]
    w_bufs, w_mode = _whole_weight(tn, n)
    vmem = (2 * (_nbytes((tm, k), yg.dtype) + _nbytes((tm, s), F32) + 2 * _nbytes((tm, tn), F32))
            + w_bufs * _nbytes((k, tn), BF16)
            + _nbytes((tm, k), BF16) + 2 * _nbytes((tm, tn), F32))
    return pl.pallas_call(
        _norm_matmul_res_kernel,
        grid=(t // tm, n // tn),
        in_specs=[pl.BlockSpec((tm, k), lambda i, j: (i, 0)),
                  pl.BlockSpec((tm, s), lambda i, j: (i, 0)),
                  pl.BlockSpec((1, k), lambda i, j: (0, 0)),
                  pl.BlockSpec((k, tn), lambda i, j: (0, j), **w_mode),
                  pl.BlockSpec((tm, tn), lambda i, j: (i, j))],
        out_specs=pl.BlockSpec((tm, tn), lambda i, j: (i, j)),
        out_shape=jax.ShapeDtypeStruct((t, n), F32),
        scratch_shapes=[pltpu.VMEM((tm, k), BF16)],
        compiler_params=_params(("parallel", "arbitrary"), vmem),
        name="norm_matmul_res",
    )(yg, ssq, g.reshape(1, k), w, res)


def _inproj_conv_kernel(x_ref, g_ref, w_ref, cw_ref, cb_ref, *rest, n_cast, z_tiles, mm_rows,
                        conv_rows):
    cast_in, o_ref = rest[:n_cast], rest[n_cast]
    cast_out = rest[n_cast + 1:2 * n_cast + 1]
    xn_ref, stage_ref = rest[2 * n_cast + 1:]
    l, tn = o_ref.shape[1], o_ref.shape[2]
    j = pl.program_id(1)
    half = SSD_CONV_W // 2

    @pl.when(j == 0)
    def _():
        rc = NORM_ROW_CHUNK

        def body(r, carry):
            rows = pl.ds(pl.multiple_of(r * rc, rc), rc)
            xn_ref[rows, :] = _rmsnorm_rows(x_ref[0, rows, :], g_ref[...]).astype(BF16)
            return carry

        lax.fori_loop(0, l // rc, body, 0)

    @pl.when(j < z_tiles)
    def _():
        o_ref[0] = jnp.dot(xn_ref[...], w_ref[...], preferred_element_type=F32).astype(o_ref.dtype)

    @pl.when(j >= z_tiles)
    def _():
        stage_ref[0:CONV_PAD_ROWS, :] = jnp.zeros((CONV_PAD_ROWS, tn), F32)
        stage_ref[l + CONV_PAD_ROWS:l + 2 * CONV_PAD_ROWS, :] = jnp.zeros((CONV_PAD_ROWS, tn), F32)
        n_mm = l // mm_rows
        for c in range(n_mm + 1):
            if c < n_mm:
                stage_ref[CONV_PAD_ROWS + c * mm_rows:CONV_PAD_ROWS + (c + 1) * mm_rows, :] = jnp.dot(
                    xn_ref[c * mm_rows:(c + 1) * mm_rows, :], w_ref[...],
                    preferred_element_type=F32)
            if c > 0:
                for r0 in range((c - 1) * mm_rows, c * mm_rows, conv_rows):
                    acc = jnp.broadcast_to(cb_ref[...], (conv_rows, tn))
                    for k in range(SSD_CONV_W):
                        lo = r0 + CONV_PAD_ROWS + k - half
                        acc = acc + stage_ref[lo:lo + conv_rows, :] * cw_ref[k:k + 1, :]
                    o_ref[0, r0:r0 + conv_rows, :] = _silu(acc).astype(o_ref.dtype)

    _run_casts(cast_in, cast_out)


def _inproj_conv(x, g, w, conv_w, conv_b, cast_next, *, batch, seq, tn, mm_rows, conv_rows):
    d = x.shape[1]
    z_tiles, nj = D_INNER // tn, SSD_ZXBC_DIM // tn
    cdim = conv_w.shape[1]
    conv_tile = lambda b, j: (0, jnp.maximum(j - z_tiles, 0))
    cast_in, cast_out, cast_shapes, cast_vmem = _cast_slabs(cast_next, batch, nj)
    stage_shape = (seq + 2 * CONV_PAD_ROWS, tn)
    vmem = (2 * (_nbytes((seq, d), F32) + _nbytes((d, tn), BF16) + _nbytes((seq, tn), BF16))
            + _nbytes((seq, d), BF16) + _nbytes(stage_shape, F32) + 6 * _nbytes((mm_rows, tn), F32)
            + cast_vmem)
    outs = pl.pallas_call(
        functools.partial(_inproj_conv_kernel, n_cast=len(cast_next), z_tiles=z_tiles,
                          mm_rows=mm_rows, conv_rows=conv_rows),
        grid=(batch, nj),
        in_specs=[pl.BlockSpec((1, seq, d), lambda b, j: (b, 0, 0)),
                  pl.BlockSpec((1, d), lambda b, j: (0, 0)),
                  pl.BlockSpec((d, tn), lambda b, j: (0, j)),
                  pl.BlockSpec((SSD_CONV_W, tn), conv_tile),
                  pl.BlockSpec((1, tn), conv_tile)] + cast_in,
        out_specs=[pl.BlockSpec((1, seq, tn), lambda b, j: (b, 0, j))] + cast_out,
        out_shape=[jax.ShapeDtypeStruct((batch, seq, SSD_ZXBC_DIM), BF16)] + cast_shapes,
        scratch_shapes=[pltpu.VMEM((seq, d), BF16), pltpu.VMEM(stage_shape, F32)],
        compiler_params=_params(("parallel", "arbitrary"), vmem),
        name="inproj_conv",
    )(x.reshape(batch, seq, d), g.reshape(1, d), w, conv_w, conv_b.reshape(1, cdim),
      *[stack for stack, _ in cast_next])
    return outs[0], outs[1:]


def _split_dot(v, m01):
    hi = v.astype(BF16)
    r1 = v - hi.astype(F32)
    mid = r1.astype(BF16)
    lo = (r1 - mid.astype(F32)).astype(BF16)
    out = jnp.dot(hi, m01, preferred_element_type=F32)
    out = out + jnp.dot(mid, m01, preferred_element_type=F32)
    return out + jnp.dot(lo, m01, preferred_element_type=F32)


def _softplus(v):
    return jnp.maximum(v, 0.0) + jnp.log1p(jnp.exp(-jnp.abs(v)))


def _log2(n):
    assert n & (n - 1) == 0
    return n.bit_length() - 1


def _ssd_prepare(reverse, dt_ref, bias_ref, alog_ref, r_cs, r_dt, t_cs, t_w, t_e, dec_ref):
    q, hg, p = SSD_CHUNK, SSD_HEADS_PER_GROUP, SSD_HEAD_DIM
    nc = dt_ref.shape[1] // q
    rows = nc * hg
    dt_all = _softplus(dt_ref[...] + bias_ref[...])
    dta_all = dt_all * (-jnp.exp(alog_ref[...]))
    stack = lambda v: jnp.concatenate([v[:, c * q:(c + 1) * q] for c in range(nc)], axis=0)
    dt_r = stack(dt_all)
    src = lax.broadcasted_iota(jnp.int32, (q, q), 0)
    dst = lax.broadcasted_iota(jnp.int32, (q, q), 1)
    cum01 = ((src >= dst) if reverse else (src <= dst)).astype(BF16)
    last_pos = 0 if reverse else q - 1
    cs = _split_dot(stack(dta_all), cum01)
    cs_last = cs[:, last_pos:last_pos + 1]
    cs_t = cs.T
    r_cs[...] = cs
    r_dt[...] = dt_r
    t_cs[...] = cs_t
    t_w[...] = (dt_r * jnp.exp(cs_last - cs)).T
    t_e[...] = jnp.exp(cs).T
    band_row = lax.shift_right_logical(lax.broadcasted_iota(jnp.int32, (rows, rows), 0), _log2(hg))
    band_col = lax.shift_right_logical(lax.broadcasted_iota(jnp.int32, (rows, rows), 1), _log2(hg))
    dec_sel = jnp.where(band_row == band_col, jnp.exp(cs_t[last_pos:last_pos + 1, :]), 0.0)
    head_of_row = lax.broadcasted_iota(jnp.int32, (rows, SSD_GROUP_DIM), 0) & (hg - 1)
    head_of_lane = lax.shift_right_logical(
        lax.broadcasted_iota(jnp.int32, (rows, SSD_GROUP_DIM), 1), _log2(p))
    dec_ref[...] = _split_dot(dec_sel, (head_of_row == head_of_lane).astype(BF16))


def _ssd_direction(reverse, x_ref, b_ref, c_ref, h_ref, r_cs, r_dt, t_cs, t_w, t_e, dec_ref, emit):
    q, hg, p, gd = SSD_CHUNK, SSD_HEADS_PER_GROUP, SSD_HEAD_DIM, SSD_GROUP_DIM
    nc = x_ref.shape[1] // q
    rows = nc * hg
    per_tile = V7X_MXU_WIDTH // p
    tgt = lax.broadcasted_iota(jnp.int32, (q, q), 0)
    src = lax.broadcasted_iota(jnp.int32, (q, q), 1)
    valid = (tgt <= src) if reverse else (tgt >= src)
    assert rows == V7X_LANES and 2 * p == V7X_LANES
    lane = lax.broadcasted_iota(jnp.int32, (q, V7X_LANES), 1)
    second_of_pair = lax.shift_right_logical(lane, _log2(p))
    tile_head = lax.shift_right_logical(
        lax.broadcasted_iota(jnp.int32, (q, V7X_MXU_WIDTH), 1), _log2(p))
    h_ref[...] = jnp.zeros(h_ref.shape, F32)

    def body(step, carry):
        chunk = (nc - 1 - step) if reverse else step
        r0 = pl.multiple_of(chunk * q, q)
        band = pl.ds(pl.multiple_of(chunk * hg, hg), hg)
        lane0 = chunk * hg

        def spread(t_ref):
            cols = t_ref[...]
            return jnp.concatenate(
                [jnp.take_along_axis(cols, lane0 + k + second_of_pair, axis=1)
                 for k in range(0, hg, 2)], axis=1)

        cs_all = t_cs[...]
        cs_rows = r_cs[band, :]
        dt_rows = r_dt[band, :]
        dec = dec_ref[band, :][0:1, :]

        x_bf = x_ref[0, pl.ds(r0, q), :]
        x_c = x_bf.astype(F32)
        b_bf = b_ref[0, pl.ds(r0, q), :]
        c_bf = c_ref[0, pl.ds(r0, q), :]
        cb = lax.dot_general(c_bf, b_bf, (((1,), (1,)), ((), ())),
                             preferred_element_type=F32)
        h_in = h_ref[...]
        y = jnp.dot(c_bf, h_in.astype(BF16), preferred_element_type=F32) * spread(t_e)
        wx = (x_c * spread(t_w)).astype(BF16)
        h_ref[...] = h_in * dec + lax.dot_general(b_bf, wx, (((0,), (0,)), ((), ())),
                                                  preferred_element_type=F32)

        y_tiles = []
        for t0 in range(0, hg, per_tile):
            ms, xs = [], []
            x_tile = x_bf[:, t0 * p:(t0 + per_tile) * p]
            for k in range(per_tile):
                h = t0 + k
                cs_tgt = jnp.take_along_axis(cs_all, jnp.broadcast_to(lane0 + h, lane.shape), axis=1)
                seg = cs_tgt - cs_rows[h:h + 1, :]
                lmat = jnp.exp(jnp.where(valid, seg, -jnp.inf))
                ms.append((cb * lmat * dt_rows[h:h + 1, :]).astype(BF16))
                xs.append(jnp.where(tile_head == k, x_tile, jnp.zeros_like(x_tile)))
            y_tiles.append(jnp.dot(jnp.concatenate(ms, axis=1), jnp.concatenate(xs, axis=0),
                                   preferred_element_type=F32))
        emit(r0, x_c, y + jnp.concatenate(y_tiles, axis=1))
        return carry

    lax.fori_loop(0, nc, body, 0, unroll=16)


def _ssd_scan_kernel(x_ref, b_ref, c_ref, z_ref, dtf_ref, dtb_ref, biasf_ref, biasb_ref,
                     alogf_ref, alogb_ref, dskip_ref, yg_ref, ssq_ref,
                     y_acc, h_ref, r_cs, r_dt, t_cs, t_w, t_e, dec_ref):
    q = SSD_CHUNK
    decay = (r_cs, r_dt, t_cs, t_w, t_e, dec_ref)

    def emit_fwd(r0, x_c, y):
        y_acc[pl.ds(r0, q), :] = y + x_c * dskip_ref[...]

    def emit_bwd(r0, x_c, y):
        yg = (y_acc[pl.ds(r0, q), :] + y) * _silu(z_ref[0, pl.ds(r0, q), :].astype(F32))
        yg_ref[0, pl.ds(r0, q), :] = yg.astype(yg_ref.dtype)
        sq = yg * yg
        folded = sq[:, 0:V7X_LANES]
        for k in range(1, SSD_GROUP_DIM // V7X_LANES):
            folded = folded + sq[:, k * V7X_LANES:(k + 1) * V7X_LANES]
        ssq_ref[0, pl.ds(r0, q), :] = folded

    _ssd_prepare(False, dtf_ref, biasf_ref, alogf_ref, *decay)
    _ssd_direction(False, x_ref, b_ref, c_ref, h_ref, *decay, emit_fwd)
    _ssd_prepare(True, dtb_ref, biasb_ref, alogb_ref, *decay)
    _ssd_direction(True, x_ref, b_ref, c_ref, h_ref, *decay, emit_bwd)


def _ssd_scan(zxbc, dt_t, dt_bias, a_log, d_skip, *, batch, seq):
    g, hg, gd, n = SSD_GROUPS, SSD_HEADS_PER_GROUP, SSD_GROUP_DIM, SSD_STATE
    xcol = D_INNER // gd
    bcol = 2 * D_INNER // n
    ccol = (2 * D_INNER + SSD_BC_DIM) // n
    col = lambda v: v.reshape(2 * SSD_HEADS, 1)
    dskip_lanes = jnp.repeat(d_skip, SSD_HEAD_DIM).reshape(1, D_INNER)
    per_dir = lambda d: pl.BlockSpec((hg, 1), lambda b, j: (d * g + j, 0))
    rows = (seq // SSD_CHUNK) * hg
    vmem = (2 * (3 * _nbytes((seq, gd), F32) + 3 * _nbytes((seq, n), F32) + 2 * _nbytes((hg, seq), F32))
            + _nbytes((seq, gd), F32) + _nbytes((n, gd), F32) + 24 * _nbytes((SSD_CHUNK, gd), F32))
    return pl.pallas_call(
        _ssd_scan_kernel,
        grid=(batch, g),
        in_specs=[pl.BlockSpec((1, seq, gd), lambda b, j: (b, 0, xcol + j)),
                  pl.BlockSpec((1, seq, n), lambda b, j: (b, 0, bcol + j)),
                  pl.BlockSpec((1, seq, n), lambda b, j: (b, 0, ccol + j)),
                  pl.BlockSpec((1, seq, gd), lambda b, j: (b, 0, j)),
                  pl.BlockSpec((hg, seq), lambda b, j: (j, b)),
                  pl.BlockSpec((hg, seq), lambda b, j: (g + j, b)),
                  per_dir(0), per_dir(1), per_dir(0), per_dir(1),
                  pl.BlockSpec((1, gd), lambda b, j: (0, j))],
        out_specs=[pl.BlockSpec((1, seq, gd), lambda b, j: (b, 0, j)),
                   pl.BlockSpec((1, seq, V7X_LANES), lambda b, j: (b, 0, j))],
        out_shape=[jax.ShapeDtypeStruct((batch, seq, D_INNER), BF16),
                   jax.ShapeDtypeStruct((batch, seq, g * V7X_LANES), F32)],
        scratch_shapes=[pltpu.VMEM((seq, gd), F32), pltpu.VMEM((n, gd), F32)]
        + [pltpu.VMEM((rows, SSD_CHUNK), F32)] * 2 + [pltpu.VMEM((SSD_CHUNK, rows), F32)] * 3
        + [pltpu.VMEM((rows, gd), F32)],
        compiler_params=_params(("parallel", "parallel"), vmem),
        name="ssd_scan",
    )(zxbc, zxbc, zxbc, zxbc, dt_t, dt_t, col(dt_bias), col(dt_bias), col(a_log), col(a_log),
      dskip_lanes)


def _ssd_mixer(x, norm_g, w_in_bf, conv_w, conv_b, dt_bias, a_log, d_skip, ssd_norm, w_out, *,
               batch, seq):
    w_dt_t = w_in_bf[:, SSD_ZXBC_DIM:].T
    zxbc, (w_out_bf,) = _inproj_conv(x, norm_g, w_in_bf, conv_w, conv_b, [w_out], batch=batch,
                                     seq=seq, tn=512, mm_rows=512, conv_rows=128)
    dt_t = _rms_matmul_t(x, norm_g, w_dt_t, tm=1024)
    yg, ssq = _ssd_scan(zxbc, dt_t, dt_bias, a_log, d_skip, batch=batch, seq=seq)
    return _norm_matmul_res(yg.reshape(batch * seq, D_INNER), ssq.reshape(batch * seq, -1),
                            ssd_norm, w_out_bf, x, tm=512, tn=D_MODEL)


def _pool_kernel(u_ref, wg_ref, sc_ref, o_ref, pad_ref, *, rows):
    l, c = u_ref.shape[1], u_ref.shape[2]
    pad_ref[0:CONV_PAD_ROWS, :] = jnp.zeros((CONV_PAD_ROWS, c), F32)
    pad_ref[l + CONV_PAD_ROWS:l + 2 * CONV_PAD_ROWS, :] = jnp.zeros((CONV_PAD_ROWS, c), F32)
    pad_ref[CONV_PAD_ROWS:l + CONV_PAD_ROWS, :] = u_ref[0].astype(F32)

    for gi, win in enumerate(POOL_WINDOWS):
        half = win // 2

        @pl.when(pl.program_id(1) == gi)
        def _(half=half):
            def body(r, carry):
                r0 = pl.multiple_of(r * rows, rows)
                t = r0 + lax.broadcasted_iota(jnp.int32, (rows, 1), 0)
                cnt = jnp.minimum(t + half, l) - jnp.maximum(t - half, 0)
                inv = 1.0 / cnt.astype(F32)
                win = pad_ref[pl.ds(r0, rows + 2 * CONV_PAD_ROWS), :]
                s = win[CONV_PAD_ROWS - half:CONV_PAD_ROWS - half + rows, :]
                for k in range(1 - half, half):
                    s = s + win[CONV_PAD_ROWS + k:CONV_PAD_ROWS + k + rows, :]
                mix = (s * inv - win[CONV_PAD_ROWS:CONV_PAD_ROWS + rows, :]).astype(BF16)
                v = jnp.dot(mix, wg_ref[0], preferred_element_type=F32) * sc_ref[...]
                o_ref[0, pl.ds(r0, rows), :] = v.astype(o_ref.dtype)
                return carry

            lax.fori_loop(0, l // rows, body, 0)


def _pool(u, w_group, scale, *, batch, seq, rows):
    ng, gd = len(POOL_WINDOWS), POOL_GROUP_DIM
    vmem = (2 * (_nbytes((seq, gd), F32) + _nbytes((gd, gd), BF16) + _nbytes((seq, gd), BF16))
            + _nbytes((seq + 2 * CONV_PAD_ROWS, gd), F32) + 8 * _nbytes((rows, gd), F32))
    return pl.pallas_call(
        functools.partial(_pool_kernel, rows=rows),
        grid=(batch, ng),
        in_specs=[pl.BlockSpec((1, seq, gd), lambda b, j: (b, 0, j)),
                  pl.BlockSpec((1, gd, gd), lambda b, j: (j, 0, 0)),
                  pl.BlockSpec((1, gd), lambda b, j: (0, j))],
        out_specs=pl.BlockSpec((1, seq, gd), lambda b, j: (b, 0, j)),
        out_shape=jax.ShapeDtypeStruct((batch, seq, ng * gd), BF16),
        scratch_shapes=[pltpu.VMEM((seq + 2 * CONV_PAD_ROWS, gd), F32)],
        compiler_params=_params(("parallel", "parallel"), vmem),
        name="pool",
    )(u, w_group, scale.reshape(1, ng * gd))


def _pool_mixer(x, norm_g, w_in_bf, w_group_bf, scale, w_out_bf, *, batch, seq):
    u, _ = _rms_matmul(x, norm_g, w_in_bf, n=D_MODEL, tm=512, tn=D_MODEL, out_dtype=BF16)
    v = _pool(u.reshape(batch, seq, D_MODEL), w_group_bf, scale, batch=batch, seq=seq, rows=128)
    return _matmul_res(v.reshape(batch * seq, D_MODEL), w_out_bf, x, tm=512, tn=D_MODEL)


def _rmsnorm_kernel(x_ref, g_ref, o_ref):
    o_ref[...] = _rmsnorm_rows(x_ref[...], g_ref[...]).astype(o_ref.dtype)


def _rmsnorm(x, g, *, tm, out_dtype):
    t, d = x.shape
    return pl.pallas_call(
        _rmsnorm_kernel,
        grid=(t // tm,),
        in_specs=[pl.BlockSpec((tm, d), lambda i: (i, 0)), pl.BlockSpec((1, d), lambda i: (0, 0))],
        out_specs=pl.BlockSpec((tm, d), lambda i: (i, 0)),
        out_shape=jax.ShapeDtypeStruct((t, d), out_dtype),
        compiler_params=_params(("parallel",), 6 * _nbytes((tm, d), F32)),
        name="rmsnorm",
    )(x, g.reshape(1, d))


def kernel(x, ffn_norm, ffn_w_gate, ffn_w_up, ffn_w_down, mix_norm, ssd_w_in, ssd_conv_w,
           ssd_conv_b, ssd_dt_bias, ssd_a_log, ssd_d, ssd_norm, ssd_w_out, pool_w_in, pool_w_group,
           pool_scale, pool_w_out, final_norm):
    batch, seq, d = x.shape
    depth = ffn_norm.shape[0]
    h = x.reshape(batch * seq, d)

    n_ffn = 2 * depth
    ffn_stacks = (ffn_w_gate, ffn_w_up, ffn_w_down)
    ffn_w = [w[0, 0].astype(BF16) for w in ffn_stacks]
    pool_w_group2 = pool_w_group.reshape(pool_w_group.shape[0], -1, POOL_GROUP_DIM)

    def ffn(h, k, w_bf, mixer_raw):
        nxt = [(w, ((k + 1) // 2, (k + 1) % 2)) for w in ffn_stacks] if k + 1 < n_ffn else []
        h, cast = _ffn(h, ffn_norm[k // 2, k % 2], *w_bf, nxt + mixer_raw, tm=1024, tf=512,
                       final_g=final_norm if k + 1 == n_ffn else None)
        return h, cast[:len(nxt)], cast[len(nxt):]

    for i in range(depth):
        j = i // 2
        if i % 2 == 0:
            h, ffn_w, (w_in_bf,) = ffn(h, 2 * i, ffn_w, [(ssd_w_in, (j,))])
            h = _ssd_mixer(h, mix_norm[i], w_in_bf, ssd_conv_w[j], ssd_conv_b[j], ssd_dt_bias[j],
                           ssd_a_log[j], ssd_d[j], ssd_norm[j], (ssd_w_out, (j,)),
                           batch=batch, seq=seq)
        else:
            h, ffn_w, pool_w = ffn(h, 2 * i, ffn_w,
                                   [(pool_w_in, (j,)), (pool_w_group2, (j,)), (pool_w_out, (j,))])
            h = _pool_mixer(h, mix_norm[i], pool_w[0], pool_w[1].reshape(pool_w_group.shape[1:]),
                            pool_scale[j], pool_w[2], batch=batch, seq=seq)
        h, ffn_w, _ = ffn(h, 2 * i + 1, ffn_w, [])
    return h.reshape(batch, seq, d)
```
